```python
import jax, jax.numpy as jnp
from jax import lax
import numpy as np

D_MODEL = 1024
BATCH = 32
SEQ = 2048
DEPTH = 1
DEC_BATCH = 8
DEC_SEQ = 4096
PAST_LEN = 128

A_HEADS = 8
A_HEAD_DIM = 64
A_WIDTH = A_HEADS * A_HEAD_DIM
DILATED_PATTERNS = ((128, 1), (512, 4), (2048, 16))
B_HEADS = 8
B_NOPE_DIM = 64
B_ROPE_DIM = 32
B_QK_DIM = B_NOPE_DIM + B_ROPE_DIM
B_V_DIM = 64
Q_LORA_RANK = 384
KV_LORA_RANK = 256
B_WIDTH = B_HEADS * B_V_DIM
MIX_WIDTH = A_WIDTH + B_WIDTH
IN_SPLIT_WIDTHS = (A_WIDTH, A_WIDTH, A_WIDTH, Q_LORA_RANK, KV_LORA_RANK, B_ROPE_DIM)
IN_WIDTH = int(sum(IN_SPLIT_WIDTHS))
IN_SPLIT_POINTS = tuple(int(v) for v in np.cumsum(IN_SPLIT_WIDTHS)[:-1])
N_EXPERTS = 32
TOP_K = 4
D_FF = 1024
SWIGLU_LIMIT = 7.0
SWIGLU_ALPHA = 1.702
ROPE_THETA = 10000.0
NORM_EPS = 1e-5
Q_BLOCK = 128
NEG_INF = -1e30

kernel_name = 'hybrid_dilated_mla_moe_encoder'


def rms_norm(x, g):
    xf = x.astype(jnp.float32)
    y = xf * lax.rsqrt(jnp.mean(xf * xf, axis=-1, keepdims=True) + NORM_EPS)
    return (y * g.astype(jnp.float32)).astype(x.dtype)


def rope_tables(seq, dim):
    inv = 1.0 / (ROPE_THETA ** (jnp.arange(0, dim, 2, dtype=jnp.float32) / dim))
    ang = jnp.arange(seq, dtype=jnp.float32)[:, None] * inv[None, :]
    return jnp.cos(ang), jnp.sin(ang)


def apply_rope(x, cos, sin):
    x1, x2 = jnp.split(x.astype(jnp.float32), 2, axis=-1)
    c = cos[None, :, None, :]
    s = sin[None, :, None, :]
    return jnp.concatenate([x1 * c - x2 * s, x2 * c + x1 * s], axis=-1).astype(x.dtype)


def banded_attention(q, k, v, n):
    G, L, H, hd = q.shape
    nb = -(-L // n)
    Lp = nb * n
    qb = jnp.pad(q, ((0, 0), (0, Lp - L), (0, 0), (0, 0))).reshape(G, nb, n, H, hd)
    pad_kv = ((0, 0), (n, Lp - L + n), (0, 0), (0, 0))
    kp = jnp.pad(k, pad_kv).reshape(G, nb + 2, n, H, hd)
    vp = jnp.pad(v, pad_kv).reshape(G, nb + 2, n, H, hd)
    kb = jnp.concatenate([kp[:, :nb], kp[:, 1:nb + 1], kp[:, 2:]], axis=2)
    vb = jnp.concatenate([vp[:, :nb], vp[:, 1:nb + 1], vp[:, 2:]], axis=2)
    s = jnp.einsum('gbqhd,gbkhd->gbhqk', qb, kb, preferred_element_type=jnp.float32) * (hd ** -0.5)
    qi = jnp.arange(n)[:, None]
    kj = jnp.arange(3 * n)[None, :]
    rel = kj - n - qi
    kpos = jnp.arange(nb)[:, None, None] * n + kj[None] - n
    valid = (jnp.abs(rel)[None] <= n) & (kpos >= 0) & (kpos < L)
    s = jnp.where(valid[None, :, None], s, NEG_INF)
    m = jnp.max(s, axis=-1, keepdims=True)
    p = jnp.exp(s - m)
    l = jnp.sum(p, axis=-1, keepdims=True)
    o = jnp.einsum('gbhqk,gbkhd->gbqhd', (p / l).astype(v.dtype), vb)
    lse = (m + jnp.log(l))[..., 0]
    o = o.reshape(G, Lp, H, hd)[:, :L]
    lse = lse.transpose(0, 1, 3, 2).reshape(G, Lp, H)[:, :L]
    return o, lse


def split_residues(t, dil):
    B, S, H, hd = t.shape
    return t.reshape(B, S // dil, dil, H, hd).transpose(0, 2, 1, 3, 4).reshape(B * dil, S // dil, H, hd)


def merge_residues(t, B, dil):
    G, Ld = t.shape[:2]
    rest = t.shape[2:]
    t = t.reshape((B, dil, Ld) + rest)
    perm = (0, 2, 1) + tuple(range(3, t.ndim))
    return t.transpose(perm).reshape((B, dil * Ld) + rest)


def dilated_attention(q, k, v):
    B = q.shape[0]
    outs = []
    lses = []
    for window, dil in DILATED_PATTERNS:
        n = window // (2 * dil)
        o, lse = banded_attention(split_residues(q, dil), split_residues(k, dil), split_residues(v, dil), n)
        outs.append(merge_residues(o, B, dil))
        lses.append(merge_residues(lse, B, dil))
    w = jax.nn.softmax(jnp.stack(lses), axis=0)
    out = jnp.einsum('pbsh,pbshd->bshd', w, jnp.stack(outs).astype(jnp.float32))
    return out.astype(q.dtype)


def dense_attention(q, k, v):
    B, S, H, dq = q.shape
    nblk = S // Q_BLOCK
    qb = q.reshape(B, nblk, Q_BLOCK, H, dq).transpose(1, 0, 2, 3, 4)
    scale = dq ** -0.5

    def block(qi):
        s = jnp.einsum('bqhd,bkhd->bhqk', qi, k, preferred_element_type=jnp.float32) * scale
        p = jax.nn.softmax(s, axis=-1)
        return jnp.einsum('bhqk,bkhd->bqhd', p.astype(v.dtype), v)

    o = lax.map(block, qb)
    return o.transpose(1, 0, 2, 3, 4).reshape(B, S, H, v.shape[-1])


def moe_ffn(h, router_w, router_b, w_gate_up, b_gate_up, w_down, b_down):
    B, S, D = h.shape
    t = h.reshape(B * S, D)
    logits = t.astype(jnp.float32) @ router_w.astype(jnp.float32) + router_b.astype(jnp.float32)
    top_v, top_i = lax.top_k(logits, TOP_K)
    gates = jax.nn.softmax(top_v, axis=-1)
    comb = jnp.sum(jax.nn.one_hot(top_i, N_EXPERTS, dtype=jnp.float32) * gates[..., None], axis=1)
    comb = comb.astype(t.dtype)
    y = jnp.zeros_like(t)
    for e in range(N_EXPERTS):
        gu = t @ w_gate_up[e] + b_gate_up[e]
        gate, up = jnp.split(gu, 2, axis=-1)
        gate = jnp.minimum(gate, SWIGLU_LIMIT)
        up = jnp.clip(up, -SWIGLU_LIMIT, SWIGLU_LIMIT)
        act = (up + 1) * gate * jax.nn.sigmoid(SWIGLU_ALPHA * gate)
        y = y + comb[:, e:e + 1] * (act @ w_down[e] + b_down[e])
    return y.reshape(B, S, D)


def encoder_layer(x, rope_a, rope_b, attn_norm, w_in, q_norm, w_uq, kv_norm, w_ukv, w_o,
                  ffn_norm, router_w, router_b, w_gate_up, b_gate_up, w_down, b_down):
    B, S, _ = x.shape
    h = rms_norm(x, attn_norm)
    proj = h @ w_in
    q_a, k_a, v_a, c_q, c_kv, k_rope = jnp.split(proj, IN_SPLIT_POINTS, axis=-1)
    q_a = apply_rope(q_a.reshape(B, S, A_HEADS, A_HEAD_DIM), *rope_a)
    k_a = apply_rope(k_a.reshape(B, S, A_HEADS, A_HEAD_DIM), *rope_a)
    v_a = v_a.reshape(B, S, A_HEADS, A_HEAD_DIM)
    o_a = dilated_attention(q_a, k_a, v_a).reshape(B, S, A_WIDTH)
    q_b = (rms_norm(c_q, q_norm) @ w_uq).reshape(B, S, B_HEADS, B_QK_DIM)
    q_nope, q_pe = jnp.split(q_b, [B_NOPE_DIM], axis=-1)
    q_pe = apply_rope(q_pe, *rope_b)
    kv = (rms_norm(c_kv, kv_norm) @ w_ukv).reshape(B, S, B_HEADS, B_NOPE_DIM + B_V_DIM)
    k_nope, v_b = jnp.split(kv, [B_NOPE_DIM], axis=-1)
    k_pe = apply_rope(k_rope[:, :, None, :], *rope_b)
    q_full = jnp.concatenate([q_nope, q_pe], axis=-1)
    k_full = jnp.concatenate([k_nope, jnp.broadcast_to(k_pe, (B, S, B_HEADS, B_ROPE_DIM))], axis=-1)
    o_b = dense_attention(q_full, k_full, v_b).reshape(B, S, B_WIDTH)
    x = x + jnp.concatenate([o_a, o_b], axis=-1) @ w_o
    x = x + moe_ffn(rms_norm(x, ffn_norm), router_w, router_b, w_gate_up, b_gate_up, w_down, b_down)
    return x


def encoder(x, attn_norm, w_in, q_norm, w_uq, kv_norm, w_ukv, w_o, ffn_norm, router_w, router_b,
            w_gate_up, b_gate_up, w_down, b_down, final_norm):
    S = x.shape[1]
    rope_a = rope_tables(S, A_HEAD_DIM)
    rope_b = rope_tables(S, B_ROPE_DIM)
    for l in range(DEPTH):
        x = encoder_layer(x, rope_a, rope_b, attn_norm[l], w_in[l], q_norm[l], w_uq[l], kv_norm[l],
                          w_ukv[l], w_o[l], ffn_norm[l], router_w[l], router_b[l], w_gate_up[l],
                          b_gate_up[l], w_down[l], b_down[l])
    return rms_norm(x, final_norm)


def setup_inputs(seed: int = 0) -> dict:
    key = jax.random.key(seed)
    ks = jax.random.split(key, 20)
    f32 = jnp.float32

    def nrm(k, shape, scale):
        return jax.random.normal(k, shape, f32) * scale

    def gain(k, shape):
        return 1.0 + 0.02 * jax.random.normal(k, shape, f32)

    return {
        'x_prompt': jax.random.normal(ks[0], (BATCH, SEQ, D_MODEL), f32),
        'x_sample': jax.random.normal(ks[1], (DEC_BATCH, DEC_SEQ, D_MODEL), f32),
        'attn_norm': gain(ks[2], (DEPTH, D_MODEL)),
        'w_in': nrm(ks[3], (DEPTH, D_MODEL, IN_WIDTH), D_MODEL ** -0.5),
        'q_norm': gain(ks[4], (DEPTH, Q_LORA_RANK)),
        'w_uq': nrm(ks[5], (DEPTH, Q_LORA_RANK, B_HEADS * B_QK_DIM), Q_LORA_RANK ** -0.5),
        'kv_norm': gain(ks[6], (DEPTH, KV_LORA_RANK)),
        'w_ukv': nrm(ks[7], (DEPTH, KV_LORA_RANK, B_HEADS * (B_NOPE_DIM + B_V_DIM)), KV_LORA_RANK ** -0.5),
        'w_o': nrm(ks[8], (DEPTH, MIX_WIDTH, D_MODEL), MIX_WIDTH ** -0.5),
        'ffn_norm': gain(ks[9], (DEPTH, D_MODEL)),
        'router_w': nrm(ks[10], (DEPTH, D_MODEL, N_EXPERTS), D_MODEL ** -0.5),
        'router_b': nrm(ks[11], (DEPTH, N_EXPERTS), 0.01),
        'w_gate_up': nrm(ks[12], (DEPTH, N_EXPERTS, D_MODEL, 2 * D_FF), D_MODEL ** -0.5),
        'b_gate_up': nrm(ks[13], (DEPTH, N_EXPERTS, 2 * D_FF), 0.01),
        'w_down': nrm(ks[14], (DEPTH, N_EXPERTS, D_FF, D_MODEL), D_FF ** -0.5),
        'b_down': nrm(ks[15], (DEPTH, N_EXPERTS, D_MODEL), 0.01),
        'final_norm': gain(ks[16], (D_MODEL,)),
    }


def reference(x_prompt, x_sample, attn_norm, w_in, q_norm, w_uq, kv_norm, w_ukv, w_o, ffn_norm,
              router_w, router_b, w_gate_up, b_gate_up, w_down, b_down, final_norm):
    y_prompt = encoder(x_prompt, attn_norm, w_in, q_norm, w_uq, kv_norm, w_ukv, w_o, ffn_norm,
                       router_w, router_b, w_gate_up, b_gate_up, w_down, b_down, final_norm)
    y_sample = encoder(x_sample, attn_norm, w_in, q_norm, w_uq, kv_norm, w_ukv, w_o, ffn_norm,
                       router_w, router_b, w_gate_up, b_gate_up, w_down, b_down, final_norm)
    return (y_prompt, y_sample)
```

```python
import functools

import jax
import jax.numpy as jnp
from jax import lax
from jax.experimental import pallas as pl
from jax.experimental.pallas import tpu as pltpu

D_MODEL = 1024
N_HEADS = 8
HEAD_DIM = 64
A_WIDTH = N_HEADS * HEAD_DIM
NOPE_DIM = 64
ROPE_DIM = 32
QK_DIM = NOPE_DIM + ROPE_DIM
V_DIM = 64
Q_RANK = 384
KV_RANK = 256
DILATIONS = (1, 4, 16)
BAND = 64
N_EXPERTS = 32
TOP_K = 4
D_FF = 1024
SWIGLU_LIMIT = 7.0
SWIGLU_ALPHA = 1.702
ROPE_THETA = 10000.0
NORM_EPS = 1e-5
NEG_INF = -1e30

LANES = 128
IN_SPLITS = (0, 512, 1024, 1536, 1920, 2176, 2304)

PROJ_TM = 512
ATTN_TQ = 256
ATTN_TK = 512
DIL_QB = 128
POST_TM = 512
DISPATCH_TM = 256
FFN_TM = 512
FFN_CHUNK = 256
COMBINE_TM = 256

F32 = jnp.float32
BF16 = jnp.bfloat16


def _vmem(mib):
    return mib * 1024 * 1024


def _rms(x, g):
    return x * lax.rsqrt(jnp.mean(x * x, axis=-1, keepdims=True) + NORM_EPS) * g


def _proj_kernel(x_ref, g_ref, win_ref, qn_ref, wuq_ref, kvn_ref, wukv_ref,
                 cosa_ref, sina_ref, cosb_ref, sinb_ref,
                 qa_ref, ka_ref, va_ref, qb_ref, kb_ref, vb_ref):
    tm = x_ref.shape[0]
    hb = _rms(x_ref[...], g_ref[...]).astype(BF16)

    def mm(g):
        return jnp.dot(hb, win_ref[:, IN_SPLITS[g]:IN_SPLITS[g + 1]], preferred_element_type=F32)

    lane_a = lax.broadcasted_iota(jnp.int32, (tm, A_WIDTH), 1)
    first_half = (lane_a % HEAD_DIM) < (HEAD_DIM // 2)
    cosa = jnp.tile(cosa_ref[...], (1, A_WIDTH // LANES))
    sina = jnp.tile(sina_ref[...], (1, A_WIDTH // LANES))

    def rope_a(t):
        sw = jnp.where(first_half, pltpu.roll(t, A_WIDTH - HEAD_DIM // 2, 1), pltpu.roll(t, HEAD_DIM // 2, 1))
        return t * cosa + sw * sina

    qa_ref[...] = (rope_a(mm(0)) * (HEAD_DIM ** -0.5)).astype(BF16)
    ka_ref[...] = rope_a(mm(1)).astype(BF16)
    va_ref[...] = mm(2).astype(BF16)

    def rope_b(t, cosb, sinb):
        w = t.shape[1]
        lane = lax.broadcasted_iota(jnp.int32, t.shape, 1) % LANES
        half = ROPE_DIM // 2
        sw = jnp.where(lane < NOPE_DIM + half, pltpu.roll(t, w - half, 1), pltpu.roll(t, half, 1))
        return t * cosb + sw * sinb

    cq = _rms(mm(3), qn_ref[...]).astype(BF16)
    qb = jnp.dot(cq, wuq_ref[...], preferred_element_type=F32)
    cosb8 = jnp.tile(cosb_ref[...], (1, N_HEADS))
    sinb8 = jnp.tile(sinb_ref[...], (1, N_HEADS))
    qb_ref[...] = (rope_b(qb, cosb8, sinb8) * (QK_DIM ** -0.5)).astype(BF16)

    ckv = _rms(mm(4), kvn_ref[...]).astype(BF16)
    kv = jnp.dot(ckv, wukv_ref[...], preferred_element_type=F32)
    kpe = rope_b(mm(5), cosb_ref[...], sinb_ref[...])
    kb_ref[...] = (kv[:, :N_HEADS * LANES] + jnp.tile(kpe, (1, N_HEADS))).astype(BF16)
    lane_v = lax.broadcasted_iota(jnp.int32, (tm, N_HEADS * LANES), 1) % LANES
    vb_ref[...] = jnp.where(lane_v == V_DIM, 1.0, kv[:, N_HEADS * LANES:]).astype(BF16)


def _proj(x2d, seq, wts, tables):
    t = x2d.shape[0]
    tm = PROJ_TM
    n_seq_tiles = seq // tm
    cosa, sina, cosb, sinb = tables
    full = lambda a: pl.BlockSpec(a.shape, lambda i: (0,) * a.ndim)
    tab = pl.BlockSpec((tm, LANES), lambda i: (i % n_seq_tiles, 0))
    row = lambda w: pl.BlockSpec((tm, w), lambda i: (i, 0))
    outs = [A_WIDTH, A_WIDTH, A_WIDTH, N_HEADS * LANES, N_HEADS * LANES, N_HEADS * LANES]
    return pl.pallas_call(
        _proj_kernel,
        grid=(t // tm,),
        in_specs=[row(D_MODEL), full(wts["attn_norm"]), full(wts["w_in"]), full(wts["q_norm"]), full(wts["w_uq"]),
                  full(wts["kv_norm"]), full(wts["w_ukv"]), tab, tab, tab, tab],
        out_specs=[row(w) for w in outs],
        out_shape=[jax.ShapeDtypeStruct((t, w), BF16) for w in outs],
        compiler_params=pltpu.CompilerParams(dimension_semantics=("arbitrary",), vmem_limit_bytes=_vmem(56)),
        name="proj",
    )(x2d, wts["attn_norm"], wts["w_in"], wts["q_norm"], wts["w_uq"], wts["kv_norm"], wts["w_ukv"],
      cosa, sina, cosb, sinb)


def _attn_b_kernel(q_ref, k_ref, v_ref, o_ref, *, seq):
    tq = q_ref.shape[1]
    outs = []
    for j in range(2):
        lanes = slice(LANES * j, LANES * (j + 1))
        q = q_ref[0, :, lanes]

        def body(i, carry, lanes=lanes, q=q):
            m, acc = carry
            rows = pl.ds(pl.multiple_of(i * ATTN_TK, ATTN_TK), ATTN_TK)
            s = lax.dot_general(q, k_ref[0, rows, lanes], (((1,), (1,)), ((), ())), preferred_element_type=F32)
            m_new = jnp.maximum(m, jnp.max(s, axis=-1, keepdims=True))
            p = jnp.exp(s - m_new)
            acc = jnp.exp(m - m_new) * acc + jnp.dot(p.astype(BF16), v_ref[0, rows, lanes], preferred_element_type=F32)
            return m_new, acc

        m0 = jnp.full((tq, 1), NEG_INF, F32)
        acc0 = jnp.zeros((tq, LANES), F32)
        _, acc = lax.fori_loop(0, seq // ATTN_TK, body, (m0, acc0))
        outs.append(acc / acc[:, V_DIM:V_DIM + 1])
    lane = lax.broadcasted_iota(jnp.int32, (tq, LANES), 1)
    o_ref[0] = jnp.where(lane < V_DIM, outs[0], pltpu.roll(outs[1], V_DIM, 1)).astype(BF16)


def _attn_b(qb, kb, vb, batch, seq):
    q3 = qb.reshape(batch, seq, N_HEADS * LANES)
    k3 = kb.reshape(batch, seq, N_HEADS * LANES)
    v3 = vb.reshape(batch, seq, N_HEADS * LANES)
    tq = ATTN_TQ
    out = pl.pallas_call(
        functools.partial(_attn_b_kernel, seq=seq),
        grid=(batch, N_HEADS // 2, seq // tq),
        in_specs=[pl.BlockSpec((1, tq, 2 * LANES), lambda b, h, i: (b, i, h)),
                  pl.BlockSpec((1, seq, 2 * LANES), lambda b, h, i: (b, 0, h)),
                  pl.BlockSpec((1, seq, 2 * LANES), lambda b, h, i: (b, 0, h))],
        out_specs=pl.BlockSpec((1, tq, LANES), lambda b, h, i: (b, i, h)),
        out_shape=jax.ShapeDtypeStruct((batch, seq, N_HEADS * V_DIM), BF16),
        compiler_params=pltpu.CompilerParams(dimension_semantics=("arbitrary",) * 3, vmem_limit_bytes=_vmem(48)),
        name="attn_b",
    )(q3, k3, v3)
    return out.reshape(batch * seq, N_HEADS * V_DIM)


def _dil_kernel(*refs, length, first, last):
    if first:
        q_ref, k_ref, v_ref, o_ref, lse_ref = refs
        op_ref = lp_ref = None
    elif last:
        q_ref, k_ref, v_ref, op_ref, lp_ref, o_ref = refs
        lse_ref = None
    else:
        q_ref, k_ref, v_ref, op_ref, lp_ref, o_ref, lse_ref = refs
    tq = q_ref.shape[1]
    qb_rows = DIL_QB
    win = min(2 * qb_rows, length)
    qi = pl.program_id(2)
    lane = lax.broadcasted_iota(jnp.int32, (qb_rows, LANES), 1)
    low = lane < HEAD_DIM
    for qb in range(tq // qb_rows):
        rows = slice(qb * qb_rows, (qb + 1) * qb_rows)
        r0 = qi * tq + qb * qb_rows
        start = pl.multiple_of(jnp.clip(r0 - BAND, 0, length - win), BAND)
        kpos = start + lax.broadcasted_iota(jnp.int32, (2 * qb_rows, win), 1)
        qpos = r0 + lax.broadcasted_iota(jnp.int32, (2 * qb_rows, win), 0) % qb_rows
        valid = jnp.abs(kpos - qpos) <= BAND
        lse_tile = jnp.zeros((qb_rows, LANES), F32)
        for hp in range(N_HEADS // 2):
            cols = slice(hp * LANES, (hp + 1) * LANES)
            qp = q_ref[0, rows, cols]
            zero = jnp.zeros_like(qp)
            qq = jnp.concatenate([jnp.where(low, qp, zero), jnp.where(low, zero, qp)], axis=0)
            kw = k_ref[0, pl.ds(start, win), cols]
            vw = v_ref[0, pl.ds(start, win), cols]
            s = lax.dot_general(qq, kw, (((1,), (1,)), ((), ())), preferred_element_type=F32)
            s = jnp.where(valid, s, NEG_INF)
            m = jnp.max(s, axis=-1, keepdims=True)
            p = jnp.exp(s - m)
            l = jnp.sum(p, axis=-1, keepdims=True)
            pv = jnp.dot(p.astype(BF16), vw, preferred_element_type=F32)
            if first:
                o_rows = pv / l
                lse = m + jnp.log(l)
            else:
                lp = lp_ref[0, rows, :]
                lse_prev = jnp.concatenate([lp[:, 2 * hp:2 * hp + 1], lp[:, 2 * hp + 1:2 * hp + 2]], axis=0)
                op = op_ref[0, rows, cols].astype(F32)
                op2 = jnp.concatenate([op, op], axis=0)
                top = jnp.maximum(lse_prev, m)
                a = jnp.exp(lse_prev - top)
                b = jnp.exp(m - top)
                den = a + b * l
                o_rows = (a * op2 + b * pv) / den
                lse = top + jnp.log(den)
            o_ref[0, rows, cols] = jnp.where(low, o_rows[:qb_rows], o_rows[qb_rows:]).astype(BF16)
            if not last:
                lse_tile = jnp.where(lane == 2 * hp, lse[:qb_rows],
                                     jnp.where(lane == 2 * hp + 1, lse[qb_rows:], lse_tile))
        if not last:
            lse_ref[0, rows, :] = lse_tile


def _dilated(qa, ka, va, batch, seq):
    o = lse = None
    for p, dil in enumerate(DILATIONS):
        first, last = p == 0, p == len(DILATIONS) - 1
        length = seq // dil
        tq = min(length, 512)
        view = lambda a, w: a.reshape(batch, length, dil * w)
        q_spec = pl.BlockSpec((1, tq, A_WIDTH), lambda b, r, i: (b, i, r))
        kv_spec = pl.BlockSpec((1, length, A_WIDTH), lambda b, r, i: (b, 0, r))
        l_spec = pl.BlockSpec((1, tq, LANES), lambda b, r, i: (b, i, r))
        args = [view(qa, A_WIDTH), view(ka, A_WIDTH), view(va, A_WIDTH)]
        in_specs = [q_spec, kv_spec, kv_spec]
        if not first:
            args += [view(o, A_WIDTH), view(lse, LANES)]
            in_specs += [q_spec, l_spec]
        out_shape = [jax.ShapeDtypeStruct((batch, length, dil * A_WIDTH), BF16)]
        out_specs = [q_spec]
        if not last:
            out_shape.append(jax.ShapeDtypeStruct((batch, length, dil * LANES), F32))
            out_specs.append(l_spec)
        res = pl.pallas_call(
            functools.partial(_dil_kernel, length=length, first=first, last=last),
            grid=(batch, dil, length // tq),
            in_specs=in_specs,
            out_specs=out_specs,
            out_shape=out_shape,
            compiler_params=pltpu.CompilerParams(dimension_semantics=("arbitrary",) * 3, vmem_limit_bytes=_vmem(48)),
            name=f"dilated_d{dil}",
        )(*args)
        o = res[0].reshape(batch * seq, A_WIDTH)
        if not last:
            lse = res[1].reshape(batch * seq, LANES)
    return o


def _post_kernel(xp_ref, oap_ref, obp_ref, xs_ref, oas_ref, obs_ref, wo_ref, g_ref, rwh_ref, rwl_ref, rb_ref,
                 x1_ref, hn_ref, route_ref, routet_ref, cout_ref, carry_ref, *, prompt_tiles):
    tm = xp_ref.shape[0]
    is_prompt = pl.program_id(0) < prompt_tiles

    @pl.when(pl.program_id(0) == 0)
    def _():
        carry_ref[...] = jnp.zeros_like(carry_ref)

    oa = jnp.where(is_prompt, oap_ref[...], oas_ref[...])
    ob = jnp.where(is_prompt, obp_ref[...], obs_ref[...])
    attn = jnp.dot(oa, wo_ref[:A_WIDTH, :], preferred_element_type=F32)
    attn += jnp.dot(ob, wo_ref[A_WIDTH:, :], preferred_element_type=F32)
    x1 = jnp.where(is_prompt, xp_ref[...], xs_ref[...]) + attn
    x1_ref[...] = x1
    hn = _rms(x1, g_ref[...])
    hn_ref[...] = hn

    hi = hn.astype(BF16)
    lo = (hn - hi.astype(F32)).astype(BF16)
    logits = jnp.dot(hi, rwh_ref[...], preferred_element_type=F32)
    logits += jnp.dot(lo, rwh_ref[...], preferred_element_type=F32)
    logits += jnp.dot(hi, rwl_ref[...], preferred_element_type=F32)
    logits += rb_ref[...]

    lane = lax.broadcasted_iota(jnp.int32, (tm, LANES), 1)
    work = logits
    vals, sels = [], []
    for _ in range(TOP_K):
        mx = jnp.max(work, axis=-1, keepdims=True)
        first = jnp.min(jnp.where(work == mx, lane, LANES), axis=-1, keepdims=True)
        sel = lane == first
        work = jnp.where(sel, -jnp.inf, work)
        vals.append(mx)
        sels.append(sel)
    exps = [jnp.exp(v - vals[0]) for v in vals]
    den = exps[0] + exps[1] + exps[2] + exps[3]

    sel_all = (sels[0] | sels[1] | sels[2] | sels[3]).astype(F32)
    tri = (lax.broadcasted_iota(jnp.int32, (tm, tm), 1) < lax.broadcasted_iota(jnp.int32, (tm, tm), 0)).astype(BF16)
    before = jnp.dot(tri, sel_all.astype(BF16), preferred_element_type=F32) + carry_ref[0:1, :]
    carry_ref[0:1, :] = carry_ref[0:1, :] + jnp.sum(sel_all, axis=0, keepdims=True)
    cout_ref[...] = carry_ref[...]

    lane_f = lane.astype(F32)
    route = jnp.zeros((tm, LANES), F32)
    for k in range(TOP_K):
        idx = jnp.sum(jnp.where(sels[k], lane_f, 0.0), axis=-1, keepdims=True)
        rank = jnp.sum(jnp.where(sels[k], before, 0.0), axis=-1, keepdims=True)
        route = jnp.where(lane == k, idx, route)
        route = jnp.where(lane == TOP_K + k, rank, route)
        route = jnp.where(lane == 2 * TOP_K + k, exps[k] / den, route)
    route_ref[...] = route
    routet_ref[...] = route.T[:16, :]


def _post(prompt, sample, wts):
    tm = POST_TM
    n_p = prompt[0].shape[0] // tm
    n_s = sample[0].shape[0] // tm
    total = (n_p + n_s) * tm
    full = lambda a: pl.BlockSpec(a.shape, lambda i: (0,) * a.ndim)
    row_p = lambda w: pl.BlockSpec((tm, w), lambda i: (jnp.minimum(i, n_p - 1), 0))
    row_s = lambda w: pl.BlockSpec((tm, w), lambda i: (jnp.maximum(i - n_p, 0), 0))
    row = lambda w: pl.BlockSpec((tm, w), lambda i: (i, 0))
    widths = (D_MODEL, A_WIDTH, A_WIDTH)
    consts = [wts["w_o"], wts["ffn_norm"], wts["router_hi"], wts["router_lo"], wts["router_b"]]
    out_shape = [jax.ShapeDtypeStruct((total, D_MODEL), F32), jax.ShapeDtypeStruct((total, D_MODEL), F32),
                 jax.ShapeDtypeStruct((total, LANES), F32), jax.ShapeDtypeStruct((16, total), F32),
                 jax.ShapeDtypeStruct((8, LANES), F32)]
    out_specs = [row(D_MODEL), row(D_MODEL), row(LANES),
                 pl.BlockSpec((16, tm), lambda i: (0, i)), pl.BlockSpec((8, LANES), lambda i: (0, 0))]
    return pl.pallas_call(
        functools.partial(_post_kernel, prompt_tiles=n_p),
        grid=(n_p + n_s,),
        in_specs=[row_p(w) for w in widths] + [row_s(w) for w in widths] + [full(c) for c in consts],
        out_specs=out_specs,
        out_shape=out_shape,
        scratch_shapes=[pltpu.VMEM((8, LANES), F32)],
        compiler_params=pltpu.CompilerParams(dimension_semantics=("arbitrary",), vmem_limit_bytes=_vmem(56)),
        name="post",
    )(*prompt, *sample, *consts)


def _row_copy(src, src_row, dst, dst_row, sem):
    return pltpu.make_async_copy(src.at[pl.ds(src_row, 1)], dst.at[pl.ds(dst_row, 1)], sem)


def _dispatch_kernel(pos_ref, hn_ref, xs_ref, sem):
    tm = DISPATCH_TM
    base = pl.program_id(0) * tm

    def issue(t, c):
        for k in range(TOP_K):
            _row_copy(hn_ref, base + t, xs_ref, pos_ref[0, 0, k * tm + t], sem).start()
        return c

    def drain(t, c):
        for k in range(TOP_K):
            _row_copy(hn_ref, 0, xs_ref, 0, sem).wait()
        return c

    lax.fori_loop(0, tm, issue, 0)
    lax.fori_loop(0, tm, drain, 0)


def _dispatch(pos_blocks, hn, rows_padded):
    tm = DISPATCH_TM
    t = hn.shape[0]
    return pl.pallas_call(
        _dispatch_kernel,
        grid=(t // tm,),
        in_specs=[pl.BlockSpec((1, 1, TOP_K * tm), lambda i: (i, 0, 0), memory_space=pltpu.SMEM),
                  pl.BlockSpec(memory_space=pl.ANY)],
        out_specs=pl.BlockSpec(memory_space=pl.ANY),
        out_shape=jax.ShapeDtypeStruct((rows_padded, D_MODEL), F32),
        scratch_shapes=[pltpu.SemaphoreType.DMA(())],
        compiler_params=pltpu.CompilerParams(dimension_semantics=("arbitrary",)),
        name="dispatch",
    )(pos_blocks, hn)


def _combine_kernel(pos_ref, ys_ref, x1_ref, route_ref, g_ref, o_ref, ybuf, sem):
    tm = COMBINE_TM

    def issue(t, c):
        for k in range(TOP_K):
            pltpu.make_async_copy(ys_ref.at[pl.ds(pos_ref[0, 0, k * tm + t], 1)], ybuf.at[k, pl.ds(t, 1)], sem).start()
        return c

    def drain(t, c):
        for k in range(TOP_K):
            pltpu.make_async_copy(ys_ref.at[pl.ds(0, 1)], ybuf.at[k, pl.ds(0, 1)], sem).wait()
        return c

    lax.fori_loop(0, tm, issue, 0)
    lax.fori_loop(0, tm, drain, 0)
    route = route_ref[...]
    y = x1_ref[...]
    for k in range(TOP_K):
        y = y + route[:, 2 * TOP_K + k:2 * TOP_K + k + 1] * ybuf[k]
    o_ref[...] = _rms(y, g_ref[...])


def _combine(pos_blocks, ys, x1, route, final_norm, row_off, rows):
    tm = COMBINE_TM
    off = row_off // tm
    return pl.pallas_call(
        _combine_kernel,
        grid=(rows // tm,),
        in_specs=[pl.BlockSpec((1, 1, TOP_K * tm), lambda i: (i + off, 0, 0), memory_space=pltpu.SMEM),
                  pl.BlockSpec(memory_space=pl.ANY),
                  pl.BlockSpec((tm, D_MODEL), lambda i: (i + off, 0)),
                  pl.BlockSpec((tm, LANES), lambda i: (i + off, 0)),
                  pl.BlockSpec((1, D_MODEL), lambda i: (0, 0))],
        out_specs=pl.BlockSpec((tm, D_MODEL), lambda i: (i, 0)),
        out_shape=jax.ShapeDtypeStruct((rows, D_MODEL), F32),
        scratch_shapes=[pltpu.VMEM((TOP_K, tm, D_MODEL), F32), pltpu.SemaphoreType.DMA(())],
        compiler_params=pltpu.CompilerParams(dimension_semantics=("arbitrary",), vmem_limit_bytes=_vmem(32)),
        name="combine",
    )(pos_blocks, ys, x1, route, final_norm)


def _ffn_kernel(tile_ref, expert_ref, lo_ref, npairs_ref, x_ref, wgu_ref, bgu_ref, wd_ref, bd_ref, y_ref):
    step = pl.program_id(0)

    @pl.when(step < npairs_ref[0])
    def _():
        xb = x_ref[...].astype(BF16)
        acc = jnp.zeros((x_ref.shape[0], D_MODEL), F32)
        for c in range(D_FF // FFN_CHUNK):
            gc = slice(c * FFN_CHUNK, (c + 1) * FFN_CHUNK)
            uc = slice(D_FF + c * FFN_CHUNK, D_FF + (c + 1) * FFN_CHUNK)
            gate = jnp.dot(xb, wgu_ref[0, :, gc], preferred_element_type=F32) + bgu_ref[0, :, gc]
            up = jnp.dot(xb, wgu_ref[0, :, uc], preferred_element_type=F32) + bgu_ref[0, :, uc]
            gate = jnp.minimum(gate, SWIGLU_LIMIT)
            up = jnp.clip(up, -SWIGLU_LIMIT, SWIGLU_LIMIT)
            act = (up + 1.0) * gate * (1.0 / (1.0 + jnp.exp(-SWIGLU_ALPHA * gate)))
            acc += jnp.dot(act.astype(BF16), wd_ref[0, gc, :], preferred_element_type=F32)
        res = acc + bd_ref[0]
        lo = lo_ref[step]

        @pl.when(lo == 0)
        def _():
            y_ref[...] = res

        @pl.when(lo > 0)
        def _():
            row = lax.broadcasted_iota(jnp.int32, (x_ref.shape[0], 1), 0)
            y_ref[...] = jnp.where(row >= lo, res, y_ref[...])


def _ffn(pairs, xs, wts):
    tm = FFN_TM
    pair_tile, pair_expert, pair_lo, n_pairs = pairs
    x_map = lambda i, pt, pe, lo, n: (pt[i], 0)
    w_map = lambda i, pt, pe, lo, n: (pe[i], 0, 0)
    grid_spec = pltpu.PrefetchScalarGridSpec(
        num_scalar_prefetch=4,
        grid=(pair_tile.shape[0],),
        in_specs=[pl.BlockSpec((tm, D_MODEL), x_map),
                  pl.BlockSpec((1, D_MODEL, 2 * D_FF), w_map),
                  pl.BlockSpec((1, 1, 2 * D_FF), w_map),
                  pl.BlockSpec((1, D_FF, D_MODEL), w_map),
                  pl.BlockSpec((1, 1, D_MODEL), w_map)],
        out_specs=pl.BlockSpec((tm, D_MODEL), x_map),
    )
    return pl.pallas_call(
        _ffn_kernel,
        grid_spec=grid_spec,
        out_shape=jax.ShapeDtypeStruct(xs.shape, F32),
        compiler_params=pltpu.CompilerParams(dimension_semantics=("arbitrary",), vmem_limit_bytes=_vmem(56)),
        name="ffn",
    )(pair_tile, pair_expert, pair_lo, n_pairs, xs, wts["w_gate_up"], wts["b_gate_up"], wts["w_down"], wts["b_down"])


def _rope_tables(seq):
    pos = jnp.arange(seq, dtype=F32)[:, None]

    def cs(dim):
        inv = 1.0 / (ROPE_THETA ** (jnp.arange(0, dim, 2, dtype=F32) / dim))
        ang = pos * inv[None, :]
        return jnp.cos(ang), jnp.sin(ang)

    ca, sa = cs(HEAD_DIM)
    cb, sb = cs(ROPE_DIM)
    one = jnp.ones((seq, NOPE_DIM), F32)
    pad1 = jnp.ones((seq, LANES - QK_DIM), F32)
    cosa = jnp.concatenate([ca, ca, ca, ca], axis=1)
    sina = jnp.concatenate([-sa, sa, -sa, sa], axis=1)
    cosb = jnp.concatenate([one, cb, cb, pad1], axis=1)
    sinb = jnp.concatenate([0.0 * one, -sb, sb, 0.0 * pad1], axis=1)
    return cosa, sina, cosb, sinb


def _prep_weights(attn_norm, w_in, q_norm, w_uq, kv_norm, w_ukv, w_o, ffn_norm, router_w, router_b,
                  w_gate_up, b_gate_up, w_down, b_down):
    w = w_in[0]
    zeros = lambda n: jnp.zeros((D_MODEL, n), F32)
    krope = jnp.concatenate([zeros(NOPE_DIM), w[:, 2176:], zeros(LANES - QK_DIM)], axis=1)
    w_in_p = jnp.concatenate([w[:, :2176], krope], axis=1).astype(BF16)
    uq = w_uq[0].reshape(Q_RANK, N_HEADS, QK_DIM)
    uq = jnp.pad(uq, ((0, 0), (0, 0), (0, LANES - QK_DIM))).reshape(Q_RANK, N_HEADS * LANES).astype(BF16)
    ukv = w_ukv[0].reshape(KV_RANK, N_HEADS, NOPE_DIM + V_DIM)
    pad = lambda a: jnp.pad(a, ((0, 0), (0, 0), (0, LANES - a.shape[2]))).reshape(KV_RANK, N_HEADS * LANES)
    ukv = jnp.concatenate([pad(ukv[:, :, :NOPE_DIM]), pad(ukv[:, :, NOPE_DIM:])], axis=1).astype(BF16)
    rw = jnp.pad(router_w[0], ((0, 0), (0, LANES - N_EXPERTS)))
    rw_hi = rw.astype(BF16)
    rw_lo = (rw - rw_hi.astype(F32)).astype(BF16)
    rb = jnp.concatenate([router_b[0], jnp.full((LANES - N_EXPERTS,), NEG_INF, F32)])[None, :]
    return {
        "attn_norm": attn_norm[0][None, :], "w_in": w_in_p, "q_norm": q_norm[0][None, :], "w_uq": uq,
        "kv_norm": kv_norm[0][None, :], "w_ukv": ukv, "w_o": w_o[0].astype(BF16), "ffn_norm": ffn_norm[0][None, :],
        "router_hi": rw_hi, "router_lo": rw_lo, "router_b": rb,
        "w_gate_up": w_gate_up[0].astype(BF16), "b_gate_up": b_gate_up[0][:, None, :],
        "w_down": w_down[0].astype(BF16), "b_down": b_down[0][:, None, :],
    }


def _routing(route_t, counts, total):
    i32 = jnp.int32
    idx = route_t[0:TOP_K].astype(i32)
    rank = route_t[TOP_K:2 * TOP_K].astype(i32)
    cnt = counts[0, :N_EXPERTS].astype(i32)
    seg_end = jnp.cumsum(cnt)
    seg_start = seg_end - cnt
    experts = jnp.arange(N_EXPERTS, dtype=i32)
    pos = rank + jnp.sum(jnp.where(idx[None] == experts[:, None, None], seg_start[:, None, None], 0), axis=0)

    first_tile = seg_start // FFN_TM
    n_per = jnp.where(cnt > 0, (seg_end - 1) // FFN_TM - first_tile + 1, 0)
    pair_end = jnp.cumsum(n_per)
    n_pairs = pair_end[-1]
    slots = total * TOP_K // FFN_TM + N_EXPERTS
    j = jnp.minimum(jnp.arange(slots, dtype=i32), n_pairs - 1)
    onehot = (jnp.sum((j[:, None] >= pair_end[None, :]).astype(i32), axis=1)[:, None] == experts[None, :]).astype(i32)
    pick = lambda table: jnp.sum(onehot * table[None, :], axis=1)
    pair_expert = pick(experts)
    pair_tile = pick(first_tile) + j - pick(pair_end - n_per)
    pair_lo = jnp.maximum(pick(seg_start) - pair_tile * FFN_TM, 0)

    def blocks(tm):
        return pos.reshape(TOP_K, total // tm, tm).transpose(1, 0, 2).reshape(total // tm, 1, TOP_K * tm)

    return blocks(DISPATCH_TM), blocks(COMBINE_TM), (pair_tile, pair_expert, pair_lo, n_pairs[None].astype(i32))


def _mixers(x, wts):
    batch, seq, _ = x.shape
    x2d = x.reshape(batch * seq, D_MODEL)
    qa, ka, va, qb, kb, vb = _proj(x2d, seq, wts, _rope_tables(seq))
    oa = _dilated(qa, ka, va, batch, seq)
    ob = _attn_b(qb, kb, vb, batch, seq)
    return x2d, oa, ob


def kernel(x_prompt, x_sample, attn_norm, w_in, q_norm, w_uq, kv_norm, w_ukv, w_o, ffn_norm, router_w, router_b,
           w_gate_up, b_gate_up, w_down, b_down, final_norm):
    wts = _prep_weights(attn_norm, w_in, q_norm, w_uq, kv_norm, w_ukv, w_o, ffn_norm, router_w, router_b,
                        w_gate_up, b_gate_up, w_down, b_down)
    sets = [x_prompt, x_sample]
    rows = [x.shape[0] * x.shape[1] for x in sets]
    total = sum(rows)
    x1, hn, route, route_t, counts = _post(_mixers(x_prompt, wts), _mixers(x_sample, wts), wts)
    pos_dispatch, pos_combine, pairs = _routing(route_t, counts, total)
    xs = _dispatch(pos_dispatch, hn, total * TOP_K)
    ys = _ffn(pairs, xs, wts)
    outs = []
    off = 0
    fnorm = final_norm[None, :]
    for x, n in zip(sets, rows):
        outs.append(_combine(pos_combine, ys, x1, route, fnorm, off, n).reshape(x.shape))
        off += n
    return tuple(outs)
```

```python
import functools

import jax
import jax.numpy as jnp
from jax import lax
from jax.experimental import pallas as pl
from jax.experimental.pallas import tpu as pltpu

D_MODEL = 1024
N_HEADS = 8
HEAD_DIM = 64
A_WIDTH = N_HEADS * HEAD_DIM
NOPE_DIM = 64
ROPE_DIM = 32
QK_DIM = NOPE_DIM + ROPE_DIM
V_DIM = 64
Q_RANK = 384
KV_RANK = 256
DILATIONS = (1, 4, 16)
BAND = 64
N_EXPERTS = 32
TOP_K = 4
D_FF = 1024
SWIGLU_LIMIT = 7.0
SWIGLU_ALPHA = 1.702
ROPE_THETA = 10000.0
NORM_EPS = 1e-5
NEG_INF = -1e30

LANES = 128
IN_SPLITS = (0, 512, 1024, 1536, 1920, 2176, 2304)

PROJ_TM = 512
ATTN_TQ = 512
ATTN_TK = 256
DIL_QB = 128
POST_TM = 512
DISPATCH_TM = 256
FFN_TM = 512
FFN_CHUNK = 256
COMBINE_TM = 256

F32 = jnp.float32
BF16 = jnp.bfloat16


def _vmem(mib):
    return mib * 1024 * 1024


def _rms(x, g):
    return x * lax.rsqrt(jnp.mean(x * x, axis=-1, keepdims=True) + NORM_EPS) * g


def _proj_kernel(x_ref, g_ref, win_ref, qn_ref, wuq_ref, kvn_ref, wukv_ref,
                 cosa_ref, sina_ref, cosb_ref, sinb_ref,
                 qa_ref, ka_ref, va_ref, qb_ref, kb_ref, vbt_ref):
    tm = x_ref.shape[0]
    hb = _rms(x_ref[...], g_ref[...]).astype(BF16)

    def mm(g):
        return jnp.dot(hb, win_ref[:, IN_SPLITS[g]:IN_SPLITS[g + 1]], preferred_element_type=F32)

    lane_a = lax.broadcasted_iota(jnp.int32, (tm, A_WIDTH), 1)
    first_half = (lane_a % HEAD_DIM) < (HEAD_DIM // 2)
    cosa = jnp.tile(cosa_ref[...], (1, A_WIDTH // LANES))
    sina = jnp.tile(sina_ref[...], (1, A_WIDTH // LANES))

    def rope_a(t):
        sw = jnp.where(first_half, pltpu.roll(t, A_WIDTH - HEAD_DIM // 2, 1), pltpu.roll(t, HEAD_DIM // 2, 1))
        return t * cosa + sw * sina

    qa_ref[...] = (rope_a(mm(0)) * (HEAD_DIM ** -0.5)).astype(BF16)
    ka_ref[...] = rope_a(mm(1)).astype(BF16)
    va_ref[...] = mm(2).astype(BF16)

    def rope_b(t, cosb, sinb):
        w = t.shape[1]
        lane = lax.broadcasted_iota(jnp.int32, t.shape, 1) % LANES
        half = ROPE_DIM // 2
        sw = jnp.where(lane < NOPE_DIM + half, pltpu.roll(t, w - half, 1), pltpu.roll(t, half, 1))
        return t * cosb + sw * sinb

    cq = _rms(mm(3), qn_ref[...]).astype(BF16)
    qb = jnp.dot(cq, wuq_ref[...], preferred_element_type=F32)
    cosb8 = jnp.tile(cosb_ref[...], (1, N_HEADS))
    sinb8 = jnp.tile(sinb_ref[...], (1, N_HEADS))
    qb_ref[...] = (rope_b(qb, cosb8, sinb8) * (QK_DIM ** -0.5)).astype(BF16)

    ckv = _rms(mm(4), kvn_ref[...]).astype(BF16)
    kv = jnp.dot(ckv, wukv_ref[...], preferred_element_type=F32)
    kpe = rope_b(mm(5), cosb_ref[...], sinb_ref[...])
    kb_ref[...] = (kv[:, :N_HEADS * LANES] + jnp.tile(kpe, (1, N_HEADS))).astype(BF16)
    lane_v = lax.broadcasted_iota(jnp.int32, (tm, N_HEADS * LANES), 1) % LANES
    vbt_ref[0] = jnp.where(lane_v == V_DIM, 1.0, kv[:, N_HEADS * LANES:]).T.astype(BF16)


def _proj(x2d, seq, wts, tables):
    t = x2d.shape[0]
    tm = PROJ_TM
    n_seq_tiles = seq // tm
    cosa, sina, cosb, sinb = tables
    full = lambda a: pl.BlockSpec(a.shape, lambda i: (0,) * a.ndim)
    tab = pl.BlockSpec((tm, LANES), lambda i: (i % n_seq_tiles, 0))
    row = lambda w: pl.BlockSpec((tm, w), lambda i: (i, 0))
    outs = [A_WIDTH, A_WIDTH, A_WIDTH, N_HEADS * LANES, N_HEADS * LANES]
    vt_spec = pl.BlockSpec((1, N_HEADS * LANES, tm), lambda i: (i // n_seq_tiles, 0, i % n_seq_tiles))
    vt_shape = jax.ShapeDtypeStruct((t // seq, N_HEADS * LANES, seq), BF16)
    return pl.pallas_call(
        _proj_kernel,
        grid=(t // tm,),
        in_specs=[row(D_MODEL), full(wts["attn_norm"]), full(wts["w_in"]), full(wts["q_norm"]), full(wts["w_uq"]),
                  full(wts["kv_norm"]), full(wts["w_ukv"]), tab, tab, tab, tab],
        out_specs=[row(w) for w in outs] + [vt_spec],
        out_shape=[jax.ShapeDtypeStruct((t, w), BF16) for w in outs] + [vt_shape],
        compiler_params=pltpu.CompilerParams(dimension_semantics=("arbitrary",), vmem_limit_bytes=_vmem(56)),
        name="proj",
    )(x2d, wts["attn_norm"], wts["w_in"], wts["q_norm"], wts["w_uq"], wts["kv_norm"], wts["w_ukv"],
      cosa, sina, cosb, sinb)


def _attn_b_kernel(q_ref, k_ref, vt_ref, o_ref, st_ref, p_ref, *, seq):
    tq = q_ref.shape[1]
    chunks = [slice(i * ATTN_TK, (i + 1) * ATTN_TK) for i in range(seq // ATTN_TK)]
    halves = []
    for j in range(2):
        lanes = slice(LANES * j, LANES * (j + 1))
        q = q_ref[0, :, lanes]
        st_ref[j] = lax.dot_general(k_ref[0, :, lanes], q, (((1,), (1,)), ((), ())), preferred_element_type=F32)
        m = jnp.full((1, tq), NEG_INF, F32)
        for rows in chunks:
            m = jnp.maximum(m, jnp.max(st_ref[j, rows, :], axis=0, keepdims=True))
        for rows in chunks:
            p_ref[j, rows, :] = jnp.exp(st_ref[j, rows, :] - m).astype(BF16)
        acc = jnp.dot(vt_ref[0, lanes, :], p_ref[j], preferred_element_type=F32)
        halves.append((acc / acc[V_DIM:V_DIM + 1, :])[:V_DIM])
    o_ref[0] = jnp.concatenate(halves, axis=0).T.astype(BF16)


def _attn_b(qb, kb, vbt, batch, seq):
    q3 = qb.reshape(batch, seq, N_HEADS * LANES)
    k3 = kb.reshape(batch, seq, N_HEADS * LANES)
    tq = ATTN_TQ
    out = pl.pallas_call(
        functools.partial(_attn_b_kernel, seq=seq),
        grid=(batch, N_HEADS // 2, seq // tq),
        in_specs=[pl.BlockSpec((1, tq, 2 * LANES), lambda b, h, i: (b, i, h)),
                  pl.BlockSpec((1, seq, 2 * LANES), lambda b, h, i: (b, 0, h)),
                  pl.BlockSpec((1, 2 * LANES, seq), lambda b, h, i: (b, h, 0))],
        out_specs=pl.BlockSpec((1, tq, LANES), lambda b, h, i: (b, i, h)),
        out_shape=jax.ShapeDtypeStruct((batch, seq, N_HEADS * V_DIM), BF16),
        scratch_shapes=[pltpu.VMEM((2, seq, tq), F32), pltpu.VMEM((2, seq, tq), BF16)],
        compiler_params=pltpu.CompilerParams(dimension_semantics=("arbitrary",) * 3, vmem_limit_bytes=_vmem(48)),
        name="attn_b",
    )(q3, k3, vbt)
    return out.reshape(batch * seq, N_HEADS * V_DIM)


def _dil_kernel(*refs, length, first, last):
    if first:
        q_ref, k_ref, v_ref, o_ref, lse_ref = refs
        op_ref = lp_ref = None
    elif last:
        q_ref, k_ref, v_ref, op_ref, lp_ref, o_ref = refs
        lse_ref = None
    else:
        q_ref, k_ref, v_ref, op_ref, lp_ref, o_ref, lse_ref = refs
    tq = q_ref.shape[1]
    qb_rows = DIL_QB
    win = min(2 * qb_rows, length)
    qi = pl.program_id(2)
    lane = lax.broadcasted_iota(jnp.int32, (qb_rows, LANES), 1)
    low = lane < HEAD_DIM
    for qb in range(tq // qb_rows):
        rows = slice(qb * qb_rows, (qb + 1) * qb_rows)
        r0 = qi * tq + qb * qb_rows
        start = pl.multiple_of(jnp.clip(r0 - BAND, 0, length - win), BAND)
        kpos = start + lax.broadcasted_iota(jnp.int32, (2 * qb_rows, win), 1)
        qpos = r0 + lax.broadcasted_iota(jnp.int32, (2 * qb_rows, win), 0) % qb_rows
        valid = jnp.abs(kpos - qpos) <= BAND
        lse_tile = jnp.zeros((qb_rows, LANES), F32)
        for hp in range(N_HEADS // 2):
            cols = slice(hp * LANES, (hp + 1) * LANES)
            qp = q_ref[0, rows, cols]
            zero = jnp.zeros_like(qp)
            qq = jnp.concatenate([jnp.where(low, qp, zero), jnp.where(low, zero, qp)], axis=0)
            kw = k_ref[0, pl.ds(start, win), cols]
            vw = v_ref[0, pl.ds(start, win), cols]
            s = lax.dot_general(qq, kw, (((1,), (1,)), ((), ())), preferred_element_type=F32)
            s = jnp.where(valid, s, NEG_INF)
            m = jnp.max(s, axis=-1, keepdims=True)
            p = jnp.exp(s - m)
            l = jnp.sum(p, axis=-1, keepdims=True)
            pv = jnp.dot(p.astype(BF16), vw, preferred_element_type=F32)
            if first:
                o_rows = pv / l
                lse = m + jnp.log(l)
            else:
                lp = lp_ref[0, rows, :]
                lse_prev = jnp.concatenate([lp[:, 2 * hp:2 * hp + 1], lp[:, 2 * hp + 1:2 * hp + 2]], axis=0)
                op = op_ref[0, rows, cols].astype(F32)
                op2 = jnp.concatenate([op, op], axis=0)
                top = jnp.maximum(lse_prev, m)
                a = jnp.exp(lse_prev - top)
                b = jnp.exp(m - top)
                den = a + b * l
                o_rows = (a * op2 + b * pv) / den
                lse = top + jnp.log(den)
            o_ref[0, rows, cols] = jnp.where(low, o_rows[:qb_rows], o_rows[qb_rows:]).astype(BF16)
            if not last:
                lse_tile = jnp.where(lane == 2 * hp, lse[:qb_rows],
                                     jnp.where(lane == 2 * hp + 1, lse[qb_rows:], lse_tile))
        if not last:
            lse_ref[0, rows, :] = lse_tile


def _dilated(qa, ka, va, batch, seq):
    o = lse = None
    for p, dil in enumerate(DILATIONS):
        first, last = p == 0, p == len(DILATIONS) - 1
        length = seq // dil
        tq = min(length, 512)
        view = lambda a, w: a.reshape(batch, length, dil * w)
        q_spec = pl.BlockSpec((1, tq, A_WIDTH), lambda b, r, i: (b, i, r))
        kv_spec = pl.BlockSpec((1, length, A_WIDTH), lambda b, r, i: (b, 0, r))
        l_spec = pl.BlockSpec((1, tq, LANES), lambda b, r, i: (b, i, r))
        args = [view(qa, A_WIDTH), view(ka, A_WIDTH), view(va, A_WIDTH)]
        in_specs = [q_spec, kv_spec, kv_spec]
        if not first:
            args += [view(o, A_WIDTH), view(lse, LANES)]
            in_specs += [q_spec, l_spec]
        out_shape = [jax.ShapeDtypeStruct((batch, length, dil * A_WIDTH), BF16)]
        out_specs = [q_spec]
        if not last:
            out_shape.append(jax.ShapeDtypeStruct((batch, length, dil * LANES), F32))
            out_specs.append(l_spec)
        res = pl.pallas_call(
            functools.partial(_dil_kernel, length=length, first=first, last=last),
            grid=(batch, dil, length // tq),
            in_specs=in_specs,
            out_specs=out_specs,
            out_shape=out_shape,
            compiler_params=pltpu.CompilerParams(dimension_semantics=("arbitrary",) * 3, vmem_limit_bytes=_vmem(48)),
            name=f"dilated_d{dil}",
        )(*args)
        o = res[0].reshape(batch * seq, A_WIDTH)
        if not last:
            lse = res[1].reshape(batch * seq, LANES)
    return o


def _post_kernel(xp_ref, oap_ref, obp_ref, xs_ref, oas_ref, obs_ref, wo_ref, g_ref, rwh_ref, rwl_ref, rb_ref,
                 x1_ref, hn_ref, route_ref, routet_ref, cout_ref, carry_ref, *, prompt_tiles):
    tm = xp_ref.shape[0]
    is_prompt = pl.program_id(0) < prompt_tiles

    @pl.when(pl.program_id(0) == 0)
    def _():
        carry_ref[...] = jnp.zeros_like(carry_ref)

    oa = jnp.where(is_prompt, oap_ref[...], oas_ref[...])
    ob = jnp.where(is_prompt, obp_ref[...], obs_ref[...])
    attn = jnp.dot(oa, wo_ref[:A_WIDTH, :], preferred_element_type=F32)
    attn += jnp.dot(ob, wo_ref[A_WIDTH:, :], preferred_element_type=F32)
    x1 = jnp.where(is_prompt, xp_ref[...], xs_ref[...]) + attn
    x1_ref[...] = x1
    hn = _rms(x1, g_ref[...])
    hn_ref[...] = hn

    hi = hn.astype(BF16)
    lo = (hn - hi.astype(F32)).astype(BF16)
    logits = jnp.dot(hi, rwh_ref[...], preferred_element_type=F32)
    logits += jnp.dot(lo, rwh_ref[...], preferred_element_type=F32)
    logits += jnp.dot(hi, rwl_ref[...], preferred_element_type=F32)
    logits += rb_ref[...]

    lane = lax.broadcasted_iota(jnp.int32, (tm, LANES), 1)
    work = logits
    vals, sels = [], []
    for _ in range(TOP_K):
        mx = jnp.max(work, axis=-1, keepdims=True)
        first = jnp.min(jnp.where(work == mx, lane, LANES), axis=-1, keepdims=True)
        sel = lane == first
        work = jnp.where(sel, -jnp.inf, work)
        vals.append(mx)
        sels.append(sel)
    exps = [jnp.exp(v - vals[0]) for v in vals]
    den = exps[0] + exps[1] + exps[2] + exps[3]

    sel_all = (sels[0] | sels[1] | sels[2] | sels[3]).astype(F32)
    tri = (lax.broadcasted_iota(jnp.int32, (tm, tm), 1) < lax.broadcasted_iota(jnp.int32, (tm, tm), 0)).astype(BF16)
    before = jnp.dot(tri, sel_all.astype(BF16), preferred_element_type=F32) + carry_ref[0:1, :]
    carry_ref[0:1, :] = carry_ref[0:1, :] + jnp.sum(sel_all, axis=0, keepdims=True)
    cout_ref[...] = carry_ref[...]

    lane_f = lane.astype(F32)
    route = jnp.zeros((tm, LANES), F32)
    for k in range(TOP_K):
        idx = jnp.sum(jnp.where(sels[k], lane_f, 0.0), axis=-1, keepdims=True)
        rank = jnp.sum(jnp.where(sels[k], before, 0.0), axis=-1, keepdims=True)
        route = jnp.where(lane == k, idx, route)
        route = jnp.where(lane == TOP_K + k, rank, route)
        route = jnp.where(lane == 2 * TOP_K + k, exps[k] / den, route)
    route_ref[...] = route
    routet_ref[...] = route.T[:16, :]


def _post(prompt, sample, wts):
    tm = POST_TM
    n_p = prompt[0].shape[0] // tm
    n_s = sample[0].shape[0] // tm
    total = (n_p + n_s) * tm
    full = lambda a: pl.BlockSpec(a.shape, lambda i: (0,) * a.ndim)
    row_p = lambda w: pl.BlockSpec((tm, w), lambda i: (jnp.minimum(i, n_p - 1), 0))
    row_s = lambda w: pl.BlockSpec((tm, w), lambda i: (jnp.maximum(i - n_p, 0), 0))
    row = lambda w: pl.BlockSpec((tm, w), lambda i: (i, 0))
    widths = (D_MODEL, A_WIDTH, A_WIDTH)
    consts = [wts["w_o"], wts["ffn_norm"], wts["router_hi"], wts["router_lo"], wts["router_b"]]
    out_shape = [jax.ShapeDtypeStruct((total, D_MODEL), F32), jax.ShapeDtypeStruct((total, D_MODEL), F32),
                 jax.ShapeDtypeStruct((total, LANES), F32), jax.ShapeDtypeStruct((16, total), F32),
                 jax.ShapeDtypeStruct((8, LANES), F32)]
    out_specs = [row(D_MODEL), row(D_MODEL), row(LANES),
                 pl.BlockSpec((16, tm), lambda i: (0, i)), pl.BlockSpec((8, LANES), lambda i: (0, 0))]
    return pl.pallas_call(
        functools.partial(_post_kernel, prompt_tiles=n_p),
        grid=(n_p + n_s,),
        in_specs=[row_p(w) for w in widths] + [row_s(w) for w in widths] + [full(c) for c in consts],
        out_specs=out_specs,
        out_shape=out_shape,
        scratch_shapes=[pltpu.VMEM((8, LANES), F32)],
        compiler_params=pltpu.CompilerParams(dimension_semantics=("arbitrary",), vmem_limit_bytes=_vmem(56)),
        name="post",
    )(*prompt, *sample, *consts)


def _row_copy(src, src_row, dst, dst_row, sem):
    return pltpu.make_async_copy(src.at[pl.ds(src_row, 1)], dst.at[pl.ds(dst_row, 1)], sem)


def _dispatch_kernel(pos_ref, hn_ref, xs_ref, sem):
    tm = DISPATCH_TM

    def issue(t, c):
        for k in range(TOP_K):
            _row_copy(hn_ref, t, xs_ref, pos_ref[0, 0, k * tm + t], sem).start()
        return c

    def drain(t, c):
        for k in range(TOP_K):
            _row_copy(hn_ref, 0, xs_ref, 0, sem).wait()
        return c

    lax.fori_loop(0, tm, issue, 0)
    lax.fori_loop(0, tm, drain, 0)


def _dispatch(pos_blocks, hn, rows_padded):
    tm = DISPATCH_TM
    t = hn.shape[0]
    return pl.pallas_call(
        _dispatch_kernel,
        grid=(t // tm,),
        in_specs=[pl.BlockSpec((1, 1, TOP_K * tm), lambda i: (i, 0, 0), memory_space=pltpu.SMEM),
                  pl.BlockSpec((tm, D_MODEL), lambda i: (i, 0))],
        out_specs=pl.BlockSpec(memory_space=pl.ANY),
        out_shape=jax.ShapeDtypeStruct((rows_padded, D_MODEL), F32),
        scratch_shapes=[pltpu.SemaphoreType.DMA(())],
        compiler_params=pltpu.CompilerParams(dimension_semantics=("arbitrary",)),
        name="dispatch",
    )(pos_blocks, hn)


def _combine_kernel(pos_ref, ys_ref, x1_ref, route_ref, g_ref, o_ref, ybuf, sem):
    tm = COMBINE_TM

    def issue(t, c):
        for k in range(TOP_K):
            pltpu.make_async_copy(ys_ref.at[pl.ds(pos_ref[0, 0, k * tm + t], 1)], ybuf.at[k, pl.ds(t, 1)], sem).start()
        return c

    def drain(t, c):
        for k in range(TOP_K):
            pltpu.make_async_copy(ys_ref.at[pl.ds(0, 1)], ybuf.at[k, pl.ds(0, 1)], sem).wait()
        return c

    lax.fori_loop(0, tm, issue, 0)
    lax.fori_loop(0, tm, drain, 0)
    route = route_ref[...]
    y = x1_ref[...]
    for k in range(TOP_K):
        y = y + route[:, 2 * TOP_K + k:2 * TOP_K + k + 1] * ybuf[k]
    o_ref[...] = _rms(y, g_ref[...])


def _combine(pos_blocks, ys, x1, route, final_norm, row_off, rows):
    tm = COMBINE_TM
    off = row_off // tm
    return pl.pallas_call(
        _combine_kernel,
        grid=(rows // tm,),
        in_specs=[pl.BlockSpec((1, 1, TOP_K * tm), lambda i: (i + off, 0, 0), memory_space=pltpu.SMEM),
                  pl.BlockSpec(memory_space=pl.ANY),
                  pl.BlockSpec((tm, D_MODEL), lambda i: (i + off, 0)),
                  pl.BlockSpec((tm, LANES), lambda i: (i + off, 0)),
                  pl.BlockSpec((1, D_MODEL), lambda i: (0, 0))],
        out_specs=pl.BlockSpec((tm, D_MODEL), lambda i: (i, 0)),
        out_shape=jax.ShapeDtypeStruct((rows, D_MODEL), F32),
        scratch_shapes=[pltpu.VMEM((TOP_K, tm, D_MODEL), F32), pltpu.SemaphoreType.DMA(())],
        compiler_params=pltpu.CompilerParams(dimension_semantics=("arbitrary",), vmem_limit_bytes=_vmem(32)),
        name="combine",
    )(pos_blocks, ys, x1, route, final_norm)


def _ffn_kernel(tile_ref, expert_ref, lo_ref, npairs_ref, x_ref, wgu_ref, bgu_ref, wd_ref, bd_ref, y_ref):
    step = pl.program_id(0)

    @pl.when(step < npairs_ref[0])
    def _():
        xb = x_ref[...].astype(BF16)
        acc = jnp.zeros((x_ref.shape[0], D_MODEL), F32)
        for c in range(D_FF // FFN_CHUNK):
            gc = slice(c * FFN_CHUNK, (c + 1) * FFN_CHUNK)
            uc = slice(D_FF + c * FFN_CHUNK, D_FF + (c + 1) * FFN_CHUNK)
            gate = jnp.dot(xb, wgu_ref[0, :, gc], preferred_element_type=F32) + bgu_ref[0, :, gc]
            up = jnp.dot(xb, wgu_ref[0, :, uc], preferred_element_type=F32) + bgu_ref[0, :, uc]
            gate = jnp.minimum(gate, SWIGLU_LIMIT)
            up = jnp.clip(up, -SWIGLU_LIMIT, SWIGLU_LIMIT)
            act = (up + 1.0) * gate * (1.0 / (1.0 + jnp.exp(-SWIGLU_ALPHA * gate)))
            acc += jnp.dot(act.astype(BF16), wd_ref[0, gc, :], preferred_element_type=F32)
        res = acc + bd_ref[0]
        lo = lo_ref[step]

        @pl.when(lo == 0)
        def _():
            y_ref[...] = res

        @pl.when(lo > 0)
        def _():
            row = lax.broadcasted_iota(jnp.int32, (x_ref.shape[0], 1), 0)
            y_ref[...] = jnp.where(row >= lo, res, y_ref[...])


def _ffn(pairs, xs, wts):
    tm = FFN_TM
    pair_tile, pair_expert, pair_lo, n_pairs = pairs
    x_map = lambda i, pt, pe, lo, n: (pt[i], 0)
    w_map = lambda i, pt, pe, lo, n: (pe[i], 0, 0)
    grid_spec = pltpu.PrefetchScalarGridSpec(
        num_scalar_prefetch=4,
        grid=(pair_tile.shape[0],),
        in_specs=[pl.BlockSpec((tm, D_MODEL), x_map),
                  pl.BlockSpec((1, D_MODEL, 2 * D_FF), w_map),
                  pl.BlockSpec((1, 1, 2 * D_FF), w_map),
                  pl.BlockSpec((1, D_FF, D_MODEL), w_map),
                  pl.BlockSpec((1, 1, D_MODEL), w_map)],
        out_specs=pl.BlockSpec((tm, D_MODEL), x_map),
    )
    return pl.pallas_call(
        _ffn_kernel,
        grid_spec=grid_spec,
        out_shape=jax.ShapeDtypeStruct(xs.shape, F32),
        compiler_params=pltpu.CompilerParams(dimension_semantics=("arbitrary",), vmem_limit_bytes=_vmem(56)),
        name="ffn",
    )(pair_tile, pair_expert, pair_lo, n_pairs, xs, wts["w_gate_up"], wts["b_gate_up"], wts["w_down"], wts["b_down"])


def _rope_tables(seq):
    pos = jnp.arange(seq, dtype=F32)[:, None]

    def cs(dim):
        inv = 1.0 / (ROPE_THETA ** (jnp.arange(0, dim, 2, dtype=F32) / dim))
        ang = pos * inv[None, :]
        return jnp.cos(ang), jnp.sin(ang)

    ca, sa = cs(HEAD_DIM)
    cb, sb = cs(ROPE_DIM)
    one = jnp.ones((seq, NOPE_DIM), F32)
    pad1 = jnp.ones((seq, LANES - QK_DIM), F32)
    cosa = jnp.concatenate([ca, ca, ca, ca], axis=1)
    sina = jnp.concatenate([-sa, sa, -sa, sa], axis=1)
    cosb = jnp.concatenate([one, cb, cb, pad1], axis=1)
    sinb = jnp.concatenate([0.0 * one, -sb, sb, 0.0 * pad1], axis=1)
    return cosa, sina, cosb, sinb


def _prep_weights(attn_norm, w_in, q_norm, w_uq, kv_norm, w_ukv, w_o, ffn_norm, router_w, router_b,
                  w_gate_up, b_gate_up, w_down, b_down):
    w = w_in[0]
    zeros = lambda n: jnp.zeros((D_MODEL, n), F32)
    krope = jnp.concatenate([zeros(NOPE_DIM), w[:, 2176:], zeros(LANES - QK_DIM)], axis=1)
    w_in_p = jnp.concatenate([w[:, :2176], krope], axis=1).astype(BF16)
    uq = w_uq[0].reshape(Q_RANK, N_HEADS, QK_DIM)
    uq = jnp.pad(uq, ((0, 0), (0, 0), (0, LANES - QK_DIM))).reshape(Q_RANK, N_HEADS * LANES).astype(BF16)
    ukv = w_ukv[0].reshape(KV_RANK, N_HEADS, NOPE_DIM + V_DIM)
    pad = lambda a: jnp.pad(a, ((0, 0), (0, 0), (0, LANES - a.shape[2]))).reshape(KV_RANK, N_HEADS * LANES)
    ukv = jnp.concatenate([pad(ukv[:, :, :NOPE_DIM]), pad(ukv[:, :, NOPE_DIM:])], axis=1).astype(BF16)
    rw = jnp.pad(router_w[0], ((0, 0), (0, LANES - N_EXPERTS)))
    rw_hi = rw.astype(BF16)
    rw_lo = (rw - rw_hi.astype(F32)).astype(BF16)
    rb = jnp.concatenate([router_b[0], jnp.full((LANES - N_EXPERTS,), NEG_INF, F32)])[None, :]
    return {
        "attn_norm": attn_norm[0][None, :], "w_in": w_in_p, "q_norm": q_norm[0][None, :], "w_uq": uq,
        "kv_norm": kv_norm[0][None, :], "w_ukv": ukv, "w_o": w_o[0].astype(BF16), "ffn_norm": ffn_norm[0][None, :],
        "router_hi": rw_hi, "router_lo": rw_lo, "router_b": rb,
        "w_gate_up": w_gate_up[0].astype(BF16), "b_gate_up": b_gate_up[0][:, None, :],
        "w_down": w_down[0].astype(BF16), "b_down": b_down[0][:, None, :],
    }


def _routing(route_t, counts, total):
    i32 = jnp.int32
    idx = route_t[0:TOP_K].astype(i32)
    rank = route_t[TOP_K:2 * TOP_K].astype(i32)
    cnt = counts[0, :N_EXPERTS].astype(i32)
    seg_end = jnp.cumsum(cnt)
    seg_start = seg_end - cnt
    experts = jnp.arange(N_EXPERTS, dtype=i32)
    pos = rank + jnp.sum(jnp.where(idx[None] == experts[:, None, None], seg_start[:, None, None], 0), axis=0)

    first_tile = seg_start // FFN_TM
    n_per = jnp.where(cnt > 0, (seg_end - 1) // FFN_TM - first_tile + 1, 0)
    pair_end = jnp.cumsum(n_per)
    n_pairs = pair_end[-1]
    slots = total * TOP_K // FFN_TM + N_EXPERTS
    j = jnp.minimum(jnp.arange(slots, dtype=i32), n_pairs - 1)
    onehot = (jnp.sum((j[:, None] >= pair_end[None, :]).astype(i32), axis=1)[:, None] == experts[None, :]).astype(i32)
    pick = lambda table: jnp.sum(onehot * table[None, :], axis=1)
    pair_expert = pick(experts)
    pair_tile = pick(first_tile) + j - pick(pair_end - n_per)
    pair_lo = jnp.maximum(pick(seg_start) - pair_tile * FFN_TM, 0)

    def blocks(tm):
        return pos.reshape(TOP_K, total // tm, tm).transpose(1, 0, 2).reshape(total // tm, 1, TOP_K * tm)

    return blocks(DISPATCH_TM), blocks(COMBINE_TM), (pair_tile, pair_expert, pair_lo, n_pairs[None].astype(i32))


def _mixers(x, wts):
    batch, seq, _ = x.shape
    x2d = x.reshape(batch * seq, D_MODEL)
    qa, ka, va, qb, kb, vbt = _proj(x2d, seq, wts, _rope_tables(seq))
    oa = _dilated(qa, ka, va, batch, seq)
    ob = _attn_b(qb, kb, vbt, batch, seq)
    return x2d, oa, ob


def kernel(x_prompt, x_sample, attn_norm, w_in, q_norm, w_uq, kv_norm, w_ukv, w_o, ffn_norm, router_w, router_b,
           w_gate_up, b_gate_up, w_down, b_down, final_norm):
    wts = _prep_weights(attn_norm, w_in, q_norm, w_uq, kv_norm, w_ukv, w_o, ffn_norm, router_w, router_b,
                        w_gate_up, b_gate_up, w_down, b_down)
    sets = [x_prompt, x_sample]
    rows = [x.shape[0] * x.shape[1] for x in sets]
    total = sum(rows)
    x1, hn, route, route_t, counts = _post(_mixers(x_prompt, wts), _mixers(x_sample, wts), wts)
    pos_dispatch, pos_combine, pairs = _routing(route_t, counts, total)
    xs = _dispatch(pos_dispatch, hn, total * TOP_K)
    ys = _ffn(pairs, xs, wts)
    outs = []
    off = 0
    fnorm = final_norm[None, :]
    for x, n in zip(sets, rows):
        outs.append(_combine(pos_combine, ys, x1, route, fnorm, off, n).reshape(x.shape))
        off += n
    return tuple(outs)
```

```python
import functools

import jax
import jax.numpy as jnp
from jax import lax
from jax.experimental import pallas as pl
from jax.experimental.pallas import tpu as pltpu

D_MODEL = 1024
N_HEADS = 8
HEAD_DIM = 64
A_WIDTH = N_HEADS * HEAD_DIM
NOPE_DIM = 64
ROPE_DIM = 32
QK_DIM = NOPE_DIM + ROPE_DIM
V_DIM = 64
Q_RANK = 384
KV_RANK = 256
DILATIONS = (1, 4, 16)
BAND = 64
N_EXPERTS = 32
TOP_K = 4
D_FF = 1024
SWIGLU_LIMIT = 7.0
SWIGLU_ALPHA = 1.702
ROPE_THETA = 10000.0
NORM_EPS = 1e-5
NEG_INF = -1e30

LANES = 128
IN_SPLITS = (0, 512, 1024, 1536, 1920, 2176, 2304)

PROJ_TM = 512
ATTN_TQ = 512
ATTN_TK = 256
DIL_QB = 128
DIL_UNROLL = 8
POST_TM = 512
DISPATCH_TM = 256
DMA_UNROLL = 4
FFN_TM = 512
FFN_CHUNK = 256
COMBINE_TM = 256

F32 = jnp.float32
BF16 = jnp.bfloat16


def _vmem(mib):
    return mib * 1024 * 1024


def _rms(x, g):
    return x * lax.rsqrt(jnp.mean(x * x, axis=-1, keepdims=True) + NORM_EPS) * g


CHUNKS = D_MODEL // LANES


def _load_token_tiles(ref, n):
    return jnp.concatenate([ref[pl.ds(c, n, stride=CHUNKS), :] for c in range(CHUNKS)], axis=1)


def _store_token_tiles(ref, x):
    for c in range(CHUNKS):
        ref[pl.ds(c, x.shape[0], stride=CHUNKS), :] = x[:, c * LANES:(c + 1) * LANES]


def _proj_kernel(x_ref, g_ref, win_ref, qn_ref, wuq_ref, kvn_ref, wukv_ref,
                 cosa_ref, sina_ref, cosb_ref, sinb_ref,
                 qa_ref, ka_ref, va_ref, qb_ref, kb_ref, vbt_ref):
    tm = x_ref.shape[0]
    hb = _rms(x_ref[...], g_ref[...]).astype(BF16)

    def mm(g):
        return jnp.dot(hb, win_ref[:, IN_SPLITS[g]:IN_SPLITS[g + 1]], preferred_element_type=F32)

    lane_a = lax.broadcasted_iota(jnp.int32, (tm, A_WIDTH), 1)
    first_half = (lane_a % HEAD_DIM) < (HEAD_DIM // 2)
    cosa = jnp.tile(cosa_ref[...], (1, A_WIDTH // LANES))
    sina = jnp.tile(sina_ref[...], (1, A_WIDTH // LANES))

    def rope_a(t):
        sw = jnp.where(first_half, pltpu.roll(t, A_WIDTH - HEAD_DIM // 2, 1), pltpu.roll(t, HEAD_DIM // 2, 1))
        return t * cosa + sw * sina

    qa_ref[...] = rope_a(mm(0)) * (HEAD_DIM ** -0.5)
    ka_ref[...] = rope_a(mm(1))
    va_ref[...] = mm(2)

    def rope_b(t, cosb, sinb):
        w = t.shape[1]
        lane = lax.broadcasted_iota(jnp.int32, t.shape, 1) % LANES
        half = ROPE_DIM // 2
        sw = jnp.where(lane < NOPE_DIM + half, pltpu.roll(t, w - half, 1), pltpu.roll(t, half, 1))
        return t * cosb + sw * sinb

    cq = _rms(mm(3), qn_ref[...]).astype(BF16)
    qb = jnp.dot(cq, wuq_ref[...], preferred_element_type=F32)
    cosb8 = jnp.tile(cosb_ref[...], (1, N_HEADS))
    sinb8 = jnp.tile(sinb_ref[...], (1, N_HEADS))
    qb_ref[...] = (rope_b(qb, cosb8, sinb8) * (QK_DIM ** -0.5)).astype(BF16)

    ckv = _rms(mm(4), kvn_ref[...]).astype(BF16)
    kv = jnp.dot(ckv, wukv_ref[...], preferred_element_type=F32)
    kpe = rope_b(mm(5), cosb_ref[...], sinb_ref[...])
    kb_ref[...] = (kv[:, :N_HEADS * LANES] + jnp.tile(kpe, (1, N_HEADS))).astype(BF16)
    lane_v = lax.broadcasted_iota(jnp.int32, (tm, N_HEADS * LANES), 1) % LANES
    vbt_ref[0] = jnp.where(lane_v == V_DIM, 1.0, kv[:, N_HEADS * LANES:]).T.astype(BF16)


def _proj(x2d, seq, wts, tables):
    t = x2d.shape[0]
    tm = PROJ_TM
    n_seq_tiles = seq // tm
    cosa, sina, cosb, sinb = tables
    full = lambda a: pl.BlockSpec(a.shape, lambda i: (0,) * a.ndim)
    tab = pl.BlockSpec((tm, LANES), lambda i: (i % n_seq_tiles, 0))
    row = lambda w: pl.BlockSpec((tm, w), lambda i: (i, 0))
    outs = [A_WIDTH, A_WIDTH, A_WIDTH, N_HEADS * LANES, N_HEADS * LANES]
    vt_spec = pl.BlockSpec((1, N_HEADS * LANES, tm), lambda i: (i // n_seq_tiles, 0, i % n_seq_tiles))
    vt_shape = jax.ShapeDtypeStruct((t // seq, N_HEADS * LANES, seq), BF16)
    return pl.pallas_call(
        _proj_kernel,
        grid=(t // tm,),
        in_specs=[row(D_MODEL), full(wts["attn_norm"]), full(wts["w_in"]), full(wts["q_norm"]), full(wts["w_uq"]),
                  full(wts["kv_norm"]), full(wts["w_ukv"]), tab, tab, tab, tab],
        out_specs=[row(w) for w in outs] + [vt_spec],
        out_shape=[jax.ShapeDtypeStruct((t, w), F32 if j < 3 else BF16) for j, w in enumerate(outs)] + [vt_shape],
        compiler_params=pltpu.CompilerParams(dimension_semantics=("arbitrary",), vmem_limit_bytes=_vmem(56)),
        name="proj",
    )(x2d, wts["attn_norm"], wts["w_in"], wts["q_norm"], wts["w_uq"], wts["kv_norm"], wts["w_ukv"],
      cosa, sina, cosb, sinb)


def _attn_b_kernel(q_ref, k_ref, vt_ref, o_ref, st_ref, p_ref, *, seq):
    tq = q_ref.shape[1]
    chunks = [slice(i * ATTN_TK, (i + 1) * ATTN_TK) for i in range(seq // ATTN_TK)]
    halves = []
    for j in range(2):
        lanes = slice(LANES * j, LANES * (j + 1))
        q = q_ref[0, :, lanes]
        st_ref[j] = lax.dot_general(k_ref[0, :, lanes], q, (((1,), (1,)), ((), ())), preferred_element_type=F32)
        m = jnp.full((1, tq), NEG_INF, F32)
        for rows in chunks:
            m = jnp.maximum(m, jnp.max(st_ref[j, rows, :], axis=0, keepdims=True))
        for rows in chunks:
            p_ref[j, rows, :] = jnp.exp(st_ref[j, rows, :] - m).astype(BF16)
        acc = jnp.dot(vt_ref[0, lanes, :], p_ref[j], preferred_element_type=F32)
        halves.append((acc / acc[V_DIM:V_DIM + 1, :])[:V_DIM])
    o_ref[0] = jnp.concatenate(halves, axis=0).T.astype(BF16)


def _attn_b(qb, kb, vbt, batch, seq):
    q3 = qb.reshape(batch, seq, N_HEADS * LANES)
    k3 = kb.reshape(batch, seq, N_HEADS * LANES)
    tq = ATTN_TQ
    out = pl.pallas_call(
        functools.partial(_attn_b_kernel, seq=seq),
        grid=(batch, N_HEADS // 2, seq // tq),
        in_specs=[pl.BlockSpec((1, tq, 2 * LANES), lambda b, h, i: (b, i, h)),
                  pl.BlockSpec((1, seq, 2 * LANES), lambda b, h, i: (b, 0, h)),
                  pl.BlockSpec((1, 2 * LANES, seq), lambda b, h, i: (b, h, 0))],
        out_specs=pl.BlockSpec((1, tq, LANES), lambda b, h, i: (b, i, h)),
        out_shape=jax.ShapeDtypeStruct((batch, seq, N_HEADS * V_DIM), BF16),
        scratch_shapes=[pltpu.VMEM((2, seq, tq), F32), pltpu.VMEM((2, seq, tq), BF16)],
        compiler_params=pltpu.CompilerParams(dimension_semantics=("arbitrary",) * 3, vmem_limit_bytes=_vmem(48)),
        name="attn_b",
    )(q3, k3, vbt)
    return out.reshape(batch * seq, N_HEADS * V_DIM)


def _dil_all_kernel(bias_ref, q_ref, k_ref, v_ref, o_ref, acc_ref, m_ref, l_ref, *, seq):
    qb_rows = DIL_QB
    lane = lax.broadcasted_iota(jnp.int32, (qb_rows, LANES), 1)
    low = lane < HEAD_DIM
    for dil in DILATIONS:
        length = seq // dil
        win = min(2 * qb_rows, length)
        nblk = length // qb_rows

        def body(n, carry, dil=dil, length=length, win=win, nblk=nblk):
            res = n // nblk
            r0 = (n % nblk) * qb_rows
            start = jnp.clip(r0 - BAND, 0, length - win)
            if dil == 1:
                qrows = pl.ds(pl.multiple_of(r0, qb_rows), qb_rows)
                krows = pl.ds(pl.multiple_of(start, BAND), win)
            else:
                qrows = pl.ds(res + dil * r0, qb_rows, stride=dil)
                krows = pl.ds(res + dil * start, win, stride=dil)
            case = jnp.where(r0 == 0, 0, jnp.where(r0 == length - qb_rows, 2, 1))
            bias = bias_ref[case, :, :win]
            qp = q_ref[qrows, :]
            zero = jnp.zeros_like(qp)
            qq = jnp.concatenate([jnp.where(low, qp, zero), jnp.where(low, zero, qp)], axis=0).astype(BF16)
            kw = k_ref[krows, :].astype(BF16)
            vw = v_ref[krows, :].astype(BF16)
            s = lax.dot_general(qq, kw, (((1,), (1,)), ((), ())), preferred_element_type=F32) + bias
            m = jnp.max(s, axis=-1, keepdims=True)
            p = jnp.exp(s - m)
            l = jnp.sum(p, axis=-1, keepdims=True)
            pv = jnp.dot(p.astype(BF16), vw, preferred_element_type=F32)
            pick =lambda t: jnp.where(low, t[:qb_rows], t[qb_rows:])
            m_new, l_new, pv_new = pick(m), pick(l), pick(pv)
            if dil == DILATIONS[0]:
                m_ref[qrows, :] = m_new
                l_ref[qrows, :] = l_new
                acc_ref[qrows, :] = pv_new
            else:
                m_old = m_ref[qrows, :]
                top = jnp.maximum(m_old, m_new)
                a = jnp.exp(m_old - top)
                b = jnp.exp(m_new - top)
                m_ref[qrows, :] = top
                l_ref[qrows, :] = a * l_ref[qrows, :] + b * l_new
                acc_ref[qrows, :] = a * acc_ref[qrows, :] + b * pv_new
            return carry

        lax.fori_loop(0, dil * nblk, body, 0, unroll=DIL_UNROLL)
    for c in range(seq // 512):
        rows = slice(c * 512, (c + 1) * 512)
        o_ref[rows, :] = (acc_ref[rows, :] / l_ref[rows, :]).astype(BF16)


def _band_bias():
    q = jnp.arange(2 * DIL_QB, dtype=jnp.int32)[None, :, None] % DIL_QB
    k = jnp.arange(2 * DIL_QB, dtype=jnp.int32)[None, None, :]
    delta = jnp.array([0, -BAND, -DIL_QB], jnp.int32)[:, None, None]
    return jnp.where(jnp.abs(k + delta - q) <= BAND, 0.0, NEG_INF).astype(F32)


def _dilated_all(qa, ka, va, batch, seq):
    view = lambda a: a.reshape(batch, seq, A_WIDTH)
    spec = pl.BlockSpec((None, seq, LANES), lambda b, h: (b, 0, h))
    bias = _band_bias()
    out = pl.pallas_call(
        functools.partial(_dil_all_kernel, seq=seq),
        grid=(batch, A_WIDTH // LANES),
        in_specs=[pl.BlockSpec(bias.shape, lambda b, h: (0, 0, 0)), spec, spec, spec],
        out_specs=spec,
        out_shape=jax.ShapeDtypeStruct((batch, seq, A_WIDTH), BF16),
        scratch_shapes=[pltpu.VMEM((seq, LANES), F32)] * 3,
        compiler_params=pltpu.CompilerParams(dimension_semantics=("arbitrary",) * 2, vmem_limit_bytes=_vmem(48)),
        name="dilated",
    )(bias, view(qa), view(ka), view(va))
    return out.reshape(batch * seq, A_WIDTH)


def _post_kernel(xp_ref, oap_ref, obp_ref, xs_ref, oas_ref, obs_ref, wo_ref, g_ref, rwh_ref, rwl_ref, rb_ref,
                 x1_ref, hn_ref, route_ref, routet_ref, cout_ref, carry_ref, *, prompt_tiles):
    tm = xp_ref.shape[0]
    is_prompt = pl.program_id(0) < prompt_tiles

    @pl.when(pl.program_id(0) == 0)
    def _():
        carry_ref[...] = jnp.zeros_like(carry_ref)

    oa = jnp.where(is_prompt, oap_ref[...], oas_ref[...])
    ob = jnp.where(is_prompt, obp_ref[...], obs_ref[...])
    attn = jnp.dot(oa, wo_ref[:A_WIDTH, :], preferred_element_type=F32)
    attn += jnp.dot(ob, wo_ref[A_WIDTH:, :], preferred_element_type=F32)
    x1 = jnp.where(is_prompt, xp_ref[...], xs_ref[...]) + attn
    x1_ref[...] = x1
    hn = _rms(x1, g_ref[...])
    _store_token_tiles(hn_ref, hn)

    hi = hn.astype(BF16)
    lo = (hn - hi.astype(F32)).astype(BF16)
    logits = jnp.dot(hi, rwh_ref[...], preferred_element_type=F32)
    logits += jnp.dot(lo, rwh_ref[...], preferred_element_type=F32)
    logits += jnp.dot(hi, rwl_ref[...], preferred_element_type=F32)
    logits += rb_ref[...]

    lane = lax.broadcasted_iota(jnp.int32, (tm, LANES), 1)
    work = logits
    vals, sels = [], []
    for _ in range(TOP_K):
        mx = jnp.max(work, axis=-1, keepdims=True)
        first = jnp.min(jnp.where(work == mx, lane, LANES), axis=-1, keepdims=True)
        sel = lane == first
        work = jnp.where(sel, -jnp.inf, work)
        vals.append(mx)
        sels.append(sel)
    exps = [jnp.exp(v - vals[0]) for v in vals]
    den = exps[0] + exps[1] + exps[2] + exps[3]

    sel_all = (sels[0] | sels[1] | sels[2] | sels[3]).astype(F32)
    tri = (lax.broadcasted_iota(jnp.int32, (tm, tm), 1) < lax.broadcasted_iota(jnp.int32, (tm, tm), 0)).astype(BF16)
    before = jnp.dot(tri, sel_all.astype(BF16), preferred_element_type=F32) + carry_ref[0:1, :]
    carry_ref[0:1, :] = carry_ref[0:1, :] + jnp.sum(sel_all, axis=0, keepdims=True)
    cout_ref[...] = carry_ref[...]

    lane_f = lane.astype(F32)
    route = jnp.zeros((tm, LANES), F32)
    for k in range(TOP_K):
        idx = jnp.sum(jnp.where(sels[k], lane_f, 0.0), axis=-1, keepdims=True)
        rank = jnp.sum(jnp.where(sels[k], before, 0.0), axis=-1, keepdims=True)
        route = jnp.where(lane == k, idx, route)
        route = jnp.where(lane == TOP_K + k, rank, route)
        route = jnp.where(lane == 2 * TOP_K + k, exps[k] / den, route)
    route_ref[...] = route
    routet_ref[...] = route.T[:16, :]


def _post(prompt, sample, wts):
    tm = POST_TM
    n_p = prompt[0].shape[0] // tm
    n_s = sample[0].shape[0] // tm
    total = (n_p + n_s) * tm
    full = lambda a: pl.BlockSpec(a.shape, lambda i: (0,) * a.ndim)
    row_p = lambda w: pl.BlockSpec((tm, w), lambda i: (jnp.minimum(i, n_p - 1), 0))
    row_s = lambda w: pl.BlockSpec((tm, w), lambda i: (jnp.maximum(i - n_p, 0), 0))
    row = lambda w: pl.BlockSpec((tm, w), lambda i: (i, 0))
    widths = (D_MODEL, A_WIDTH, A_WIDTH)
    consts = [wts["w_o"], wts["ffn_norm"], wts["router_hi"], wts["router_lo"], wts["router_b"]]
    out_shape = [jax.ShapeDtypeStruct((total, D_MODEL), F32), jax.ShapeDtypeStruct((total * CHUNKS, LANES), F32),
                 jax.ShapeDtypeStruct((total, LANES), F32), jax.ShapeDtypeStruct((16, total), F32),
                 jax.ShapeDtypeStruct((8, LANES), F32)]
    out_specs = [row(D_MODEL), pl.BlockSpec((tm * CHUNKS, LANES), lambda i: (i, 0)), row(LANES),
                 pl.BlockSpec((16, tm), lambda i: (0, i)), pl.BlockSpec((8, LANES), lambda i: (0, 0))]
    return pl.pallas_call(
        functools.partial(_post_kernel, prompt_tiles=n_p),
        grid=(n_p + n_s,),
        in_specs=[row_p(w) for w in widths] + [row_s(w) for w in widths] + [full(c) for c in consts],
        out_specs=out_specs,
        out_shape=out_shape,
        scratch_shapes=[pltpu.VMEM((8, LANES), F32)],
        compiler_params=pltpu.CompilerParams(dimension_semantics=("arbitrary",), vmem_limit_bytes=_vmem(56)),
        name="post",
    )(*prompt, *sample, *consts)


def _dispatch_kernel(pos_ref, hn_ref, xs_ref, sem):
    tm = DISPATCH_TM

    def copy(t, slot):
        return pltpu.make_async_copy(hn_ref.at[t], xs_ref.at[slot], sem)

    def issue(t, c):
        for k in range(TOP_K):
            copy(t, pos_ref[0, 0, k * tm + t]).start()
        return c

    def drain(t, c):
        for k in range(TOP_K):
            copy(0, 0).wait()
        return c

    lax.fori_loop(0, tm, issue, 0, unroll=DMA_UNROLL)
    lax.fori_loop(0, tm, drain, 0, unroll=DMA_UNROLL)


def _dispatch(pos_blocks, hn, rows):
    tm = DISPATCH_TM
    hn3 = hn.reshape(-1, CHUNKS, LANES)
    return pl.pallas_call(
        _dispatch_kernel,
        grid=(hn3.shape[0] // tm,),
        in_specs=[pl.BlockSpec((1, 1, TOP_K * tm), lambda i: (i, 0, 0), memory_space=pltpu.SMEM),
                  pl.BlockSpec((tm, CHUNKS, LANES), lambda i: (i, 0, 0))],
        out_specs=pl.BlockSpec(memory_space=pl.ANY),
        out_shape=jax.ShapeDtypeStruct((rows, CHUNKS, LANES), F32),
        scratch_shapes=[pltpu.SemaphoreType.DMA(())],
        compiler_params=pltpu.CompilerParams(dimension_semantics=("arbitrary",)),
        name="dispatch",
    )(pos_blocks, hn3)


def _combine_kernel(pos_ref, ys_ref, x1_ref, route_ref, g_ref, o_ref, ybuf, sem):
    tm = COMBINE_TM

    def copy(slot, k, t):
        dst = ybuf.at[k, pl.ds(pl.multiple_of(t * CHUNKS, CHUNKS), CHUNKS)]
        return pltpu.make_async_copy(ys_ref.at[slot], dst, sem)

    def issue(t, c):
        for k in range(TOP_K):
            copy(pos_ref[0, 0, k * tm + t], k, t).start()
        return c

    def drain(t, c):
        for k in range(TOP_K):
            copy(0, k, 0).wait()
        return c

    lax.fori_loop(0, tm, issue, 0, unroll=DMA_UNROLL)
    lax.fori_loop(0, tm, drain, 0, unroll=DMA_UNROLL)
    route = route_ref[...]
    y = x1_ref[...]
    for k in range(TOP_K):
        y = y + route[:, 2 * TOP_K + k:2 * TOP_K + k + 1] * _load_token_tiles(ybuf.at[k], tm)
    o_ref[...] = _rms(y, g_ref[...])


def _combine(pos_blocks, ys, x1, route, final_norm, row_off, rows):
    tm = COMBINE_TM
    off = row_off // tm
    return pl.pallas_call(
        _combine_kernel,
        grid=(rows // tm,),
        in_specs=[pl.BlockSpec((1, 1, TOP_K * tm), lambda i: (i + off, 0, 0), memory_space=pltpu.SMEM),
                  pl.BlockSpec(memory_space=pl.ANY),
                  pl.BlockSpec((tm, D_MODEL), lambda i: (i + off, 0)),
                  pl.BlockSpec((tm, LANES), lambda i: (i + off, 0)),
                  pl.BlockSpec((1, D_MODEL), lambda i: (0, 0))],
        out_specs=pl.BlockSpec((tm, D_MODEL), lambda i: (i, 0)),
        out_shape=jax.ShapeDtypeStruct((rows, D_MODEL), F32),
        scratch_shapes=[pltpu.VMEM((TOP_K, tm * CHUNKS, LANES), F32), pltpu.SemaphoreType.DMA(())],
        compiler_params=pltpu.CompilerParams(dimension_semantics=("arbitrary",), vmem_limit_bytes=_vmem(32)),
        name="combine",
    )(pos_blocks, ys, x1, route, final_norm)


def _ffn_kernel(tile_ref, expert_ref, lo_ref, npairs_ref, x_ref, wgu_ref, bgu_ref, wd_ref, bd_ref, y_ref):
    step = pl.program_id(0)

    @pl.when(step < npairs_ref[0])
    def _():
        tm = FFN_TM
        xb = _load_token_tiles(x_ref, tm).astype(BF16)
        acc = jnp.zeros((tm, D_MODEL), F32)
        for c in range(D_FF // FFN_CHUNK):
            gc = slice(c * FFN_CHUNK, (c + 1) * FFN_CHUNK)
            uc = slice(D_FF + c * FFN_CHUNK, D_FF + (c + 1) * FFN_CHUNK)
            gate = jnp.dot(xb, wgu_ref[0, :, gc], preferred_element_type=F32) + bgu_ref[0, :, gc]
            up = jnp.dot(xb, wgu_ref[0, :, uc], preferred_element_type=F32) + bgu_ref[0, :, uc]
            gate = jnp.minimum(gate, SWIGLU_LIMIT)
            up = jnp.clip(up, -SWIGLU_LIMIT, SWIGLU_LIMIT)
            act = (up + 1.0) * gate * (1.0 / (1.0 + jnp.exp(-SWIGLU_ALPHA * gate)))
            acc += jnp.dot(act.astype(BF16), wd_ref[0, gc, :], preferred_element_type=F32)
        res = acc + bd_ref[0]
        lo = lo_ref[step]

        @pl.when(lo == 0)
        def _():
            _store_token_tiles(y_ref, res)

        @pl.when(lo > 0)
        def _():
            row = lax.broadcasted_iota(jnp.int32, (tm, 1), 0)
            _store_token_tiles(y_ref, jnp.where(row >= lo, res, _load_token_tiles(y_ref, tm)))


def _ffn(pairs, xs, wts):
    tm = FFN_TM
    pair_tile, pair_expert, pair_lo, n_pairs = pairs
    x_map = lambda i, pt, pe, lo, n: (pt[i], 0)
    w_map = lambda i, pt, pe, lo, n: (pe[i], 0, 0)
    grid_spec = pltpu.PrefetchScalarGridSpec(
        num_scalar_prefetch=4,
        grid=(pair_tile.shape[0],),
        in_specs=[pl.BlockSpec((tm * CHUNKS, LANES), x_map),
                  pl.BlockSpec((1, D_MODEL, 2 * D_FF), w_map),
                  pl.BlockSpec((1, 1, 2 * D_FF), w_map),
                  pl.BlockSpec((1, D_FF, D_MODEL), w_map),
                  pl.BlockSpec((1, 1, D_MODEL), w_map)],
        out_specs=pl.BlockSpec((tm * CHUNKS, LANES), x_map),
    )
    ys = pl.pallas_call(
        _ffn_kernel,
        grid_spec=grid_spec,
        out_shape=jax.ShapeDtypeStruct((xs.shape[0] * CHUNKS, LANES), F32),
        compiler_params=pltpu.CompilerParams(dimension_semantics=("arbitrary",), vmem_limit_bytes=_vmem(56)),
        name="ffn",
    )(pair_tile, pair_expert, pair_lo, n_pairs, xs.reshape(-1, LANES), wts["w_gate_up"], wts["b_gate_up"],
      wts["w_down"], wts["b_down"])
    return ys.reshape(xs.shape)


def _rope_tables(seq):
    pos = jnp.arange(seq, dtype=F32)[:, None]

    def cs(dim):
        inv = 1.0 / (ROPE_THETA ** (jnp.arange(0, dim, 2, dtype=F32) / dim))
        ang = pos * inv[None, :]
        return jnp.cos(ang), jnp.sin(ang)

    ca, sa = cs(HEAD_DIM)
    cb, sb = cs(ROPE_DIM)
    one = jnp.ones((seq, NOPE_DIM), F32)
    pad1 = jnp.ones((seq, LANES - QK_DIM), F32)
    cosa = jnp.concatenate([ca, ca, ca, ca], axis=1)
    sina = jnp.concatenate([-sa, sa, -sa, sa], axis=1)
    cosb = jnp.concatenate([one, cb, cb, pad1], axis=1)
    sinb = jnp.concatenate([0.0 * one, -sb, sb, 0.0 * pad1], axis=1)
    return cosa, sina, cosb, sinb


def _prep_weights(attn_norm, w_in, q_norm, w_uq, kv_norm, w_ukv, w_o, ffn_norm, router_w, router_b,
                  w_gate_up, b_gate_up, w_down, b_down):
    w = w_in[0]
    zeros = lambda n: jnp.zeros((D_MODEL, n), F32)
    krope = jnp.concatenate([zeros(NOPE_DIM), w[:, 2176:], zeros(LANES - QK_DIM)], axis=1)
    w_in_p = jnp.concatenate([w[:, :2176], krope], axis=1).astype(BF16)
    uq = w_uq[0].reshape(Q_RANK, N_HEADS, QK_DIM)
    uq = jnp.pad(uq, ((0, 0), (0, 0), (0, LANES - QK_DIM))).reshape(Q_RANK, N_HEADS * LANES).astype(BF16)
    ukv = w_ukv[0].reshape(KV_RANK, N_HEADS, NOPE_DIM + V_DIM)
    pad = lambda a: jnp.pad(a, ((0, 0), (0, 0), (0, LANES - a.shape[2]))).reshape(KV_RANK, N_HEADS * LANES)
    ukv = jnp.concatenate([pad(ukv[:, :, :NOPE_DIM]), pad(ukv[:, :, NOPE_DIM:])], axis=1).astype(BF16)
    rw = jnp.pad(router_w[0], ((0, 0), (0, LANES - N_EXPERTS)))
    rw_hi = rw.astype(BF16)
    rw_lo = (rw - rw_hi.astype(F32)).astype(BF16)
    rb = jnp.concatenate([router_b[0], jnp.full((LANES - N_EXPERTS,), NEG_INF, F32)])[None, :]
    return {
        "attn_norm": attn_norm[0][None, :], "w_in": w_in_p, "q_norm": q_norm[0][None, :], "w_uq": uq,
        "kv_norm": kv_norm[0][None, :], "w_ukv": ukv, "w_o": w_o[0].astype(BF16), "ffn_norm": ffn_norm[0][None, :],
        "router_hi": rw_hi, "router_lo": rw_lo, "router_b": rb,
        "w_gate_up": w_gate_up[0].astype(BF16), "b_gate_up": b_gate_up[0][:, None, :],
        "w_down": w_down[0].astype(BF16), "b_down": b_down[0][:, None, :],
    }


def _routing(route_t, counts, total):
    i32 = jnp.int32
    idx = route_t[0:TOP_K].astype(i32)
    rank = route_t[TOP_K:2 * TOP_K].astype(i32)
    cnt = counts[0, :N_EXPERTS].astype(i32)
    seg_end = jnp.cumsum(cnt)
    seg_start = seg_end - cnt
    experts = jnp.arange(N_EXPERTS, dtype=i32)
    pos = rank + jnp.sum(jnp.where(idx[None] == experts[:, None, None], seg_start[:, None, None], 0), axis=0)

    first_tile = seg_start // FFN_TM
    n_per = jnp.where(cnt > 0, (seg_end - 1) // FFN_TM - first_tile + 1, 0)
    pair_end = jnp.cumsum(n_per)
    n_pairs = pair_end[-1]
    slots = total * TOP_K // FFN_TM + N_EXPERTS
    j = jnp.minimum(jnp.arange(slots, dtype=i32), n_pairs - 1)
    onehot = (jnp.sum((j[:, None] >= pair_end[None, :]).astype(i32), axis=1)[:, None] == experts[None, :]).astype(i32)
    pick = lambda table: jnp.sum(onehot * table[None, :], axis=1)
    pair_expert = pick(experts)
    pair_tile = pick(first_tile) + j - pick(pair_end - n_per)
    pair_lo = jnp.maximum(pick(seg_start) - pair_tile * FFN_TM, 0)

    def blocks(tm):
        return pos.reshape(TOP_K, total // tm, tm).transpose(1, 0, 2).reshape(total // tm, 1, TOP_K * tm)

    return blocks(DISPATCH_TM), blocks(COMBINE_TM), (pair_tile, pair_expert, pair_lo, n_pairs[None].astype(i32))


def _mixers(x, wts):
    batch, seq, _ = x.shape
    x2d = x.reshape(batch * seq, D_MODEL)
    qa, ka, va, qb, kb, vbt = _proj(x2d, seq, wts, _rope_tables(seq))
    oa = _dilated_all(qa, ka, va, batch, seq)
    ob = _attn_b(qb, kb, vbt, batch, seq)
    return x2d, oa, ob


def kernel(x_prompt, x_sample, attn_norm, w_in, q_norm, w_uq, kv_norm, w_ukv, w_o, ffn_norm, router_w, router_b,
           w_gate_up, b_gate_up, w_down, b_down, final_norm):
    wts = _prep_weights(attn_norm, w_in, q_norm, w_uq, kv_norm, w_ukv, w_o, ffn_norm, router_w, router_b,
                        w_gate_up, b_gate_up, w_down, b_down)
    sets = [x_prompt, x_sample]
    rows = [x.shape[0] * x.shape[1] for x in sets]
    total = sum(rows)
    x1, hn, route, route_t, counts = _post(_mixers(x_prompt, wts), _mixers(x_sample, wts), wts)
    pos_dispatch, pos_combine, pairs = _routing(route_t, counts, total)
    xs = _dispatch(pos_dispatch, hn, total * TOP_K)
    ys = _ffn(pairs, xs, wts)
    outs = []
    off = 0
    fnorm = final_norm[None, :]
    for x, n in zip(sets, rows):
        outs.append(_combine(pos_combine, ys, x1, route, fnorm, off, n).reshape(x.shape))
        off += n
    return tuple(outs)
```

```python
import functools

import jax
import jax.numpy as jnp
from jax import lax
from jax.experimental import pallas as pl
from jax.experimental.pallas import tpu as pltpu

D_MODEL = 1024
N_HEADS = 8
HEAD_DIM = 64
A_WIDTH = N_HEADS * HEAD_DIM
NOPE_DIM = 64
ROPE_DIM = 32
QK_DIM = NOPE_DIM + ROPE_DIM
V_DIM = 64
Q_RANK = 384
KV_RANK = 256
DILATIONS = (1, 4, 16)
BAND = 64
N_EXPERTS = 32
TOP_K = 4
D_FF = 1024
SWIGLU_LIMIT = 7.0
SWIGLU_ALPHA = 1.702
ROPE_THETA = 10000.0
NORM_EPS = 1e-5
NEG_INF = -1e30

LANES = 128
IN_SPLITS = (0, 512, 1024, 1536, 1920, 2176, 2304)

PROJ_TM = 512
ATTN_TQ = 512
ATTN_TK = 256
DIL_QB = 128
DIL_UNROLL = 8
POST_TM = 512
DISPATCH_TM = 256
DMA_UNROLL = 4
FFN_TM = 512
FFN_SUB = 2
COMBINE_TM = 256

F32 = jnp.float32
BF16 = jnp.bfloat16


def _vmem(mib):
    return mib * 1024 * 1024


def _rms(x, g):
    return x * lax.rsqrt(jnp.mean(x * x, axis=-1, keepdims=True) + NORM_EPS) * g


CHUNKS = D_MODEL // LANES


def _load_token_tiles(ref, n, first=0):
    rows = lambda c: pl.ds(first * CHUNKS + c, n, stride=CHUNKS)
    return jnp.concatenate([ref[rows(c), :] for c in range(CHUNKS)], axis=1)


def _store_token_tiles(ref, x, first=0):
    for c in range(CHUNKS):
        ref[pl.ds(first * CHUNKS + c, x.shape[0], stride=CHUNKS), :] = x[:, c * LANES:(c + 1) * LANES]


def _proj_kernel(x_ref, g_ref, win_ref, qn_ref, wuq_ref, kvn_ref, wukv_ref,
                 cosa_ref, sina_ref, cosb_ref, sinb_ref,
                 qa_ref, ka_ref, va_ref, qb_ref, kb_ref, vbt_ref):
    tm = x_ref.shape[0]
    hb = _rms(x_ref[...], g_ref[...]).astype(BF16)

    def mm(g):
        return jnp.dot(hb, win_ref[:, IN_SPLITS[g]:IN_SPLITS[g + 1]], preferred_element_type=F32)

    lane_a = lax.broadcasted_iota(jnp.int32, (tm, A_WIDTH), 1)
    first_half = (lane_a % HEAD_DIM) < (HEAD_DIM // 2)
    cosa = jnp.tile(cosa_ref[...], (1, A_WIDTH // LANES))
    sina = jnp.tile(sina_ref[...], (1, A_WIDTH // LANES))

    def rope_a(t):
        sw = jnp.where(first_half, pltpu.roll(t, A_WIDTH - HEAD_DIM // 2, 1), pltpu.roll(t, HEAD_DIM // 2, 1))
        return t * cosa + sw * sina

    qa_ref[...] = rope_a(mm(0)) * (HEAD_DIM ** -0.5)
    ka_ref[...] = rope_a(mm(1))
    va_ref[...] = mm(2)

    def rope_b(t, cosb, sinb):
        w = t.shape[1]
        lane = lax.broadcasted_iota(jnp.int32, t.shape, 1) % LANES
        half = ROPE_DIM // 2
        sw = jnp.where(lane < NOPE_DIM + half, pltpu.roll(t, w - half, 1), pltpu.roll(t, half, 1))
        return t * cosb + sw * sinb

    cq = _rms(mm(3), qn_ref[...]).astype(BF16)
    qb = jnp.dot(cq, wuq_ref[...], preferred_element_type=F32)
    cosb8 = jnp.tile(cosb_ref[...], (1, N_HEADS))
    sinb8 = jnp.tile(sinb_ref[...], (1, N_HEADS))
    qb_ref[...] = (rope_b(qb, cosb8, sinb8) * (QK_DIM ** -0.5)).astype(BF16)

    ckv = _rms(mm(4), kvn_ref[...]).astype(BF16)
    kv = jnp.dot(ckv, wukv_ref[...], preferred_element_type=F32)
    kpe = rope_b(mm(5), cosb_ref[...], sinb_ref[...])
    kb_ref[...] = (kv[:, :N_HEADS * LANES] + jnp.tile(kpe, (1, N_HEADS))).astype(BF16)
    lane_v = lax.broadcasted_iota(jnp.int32, (tm, N_HEADS * LANES), 1) % LANES
    vbt_ref[0] = jnp.where(lane_v == V_DIM, 1.0, kv[:, N_HEADS * LANES:]).T.astype(BF16)


def _proj(x2d, seq, wts, tables):
    t = x2d.shape[0]
    tm = PROJ_TM
    n_seq_tiles = seq // tm
    cosa, sina, cosb, sinb = tables
    full = lambda a: pl.BlockSpec(a.shape, lambda i: (0,) * a.ndim)
    tab = pl.BlockSpec((tm, LANES), lambda i: (i % n_seq_tiles, 0))
    row = lambda w: pl.BlockSpec((tm, w), lambda i: (i, 0))
    outs = [A_WIDTH, A_WIDTH, A_WIDTH, N_HEADS * LANES, N_HEADS * LANES]
    vt_spec = pl.BlockSpec((1, N_HEADS * LANES, tm), lambda i: (i // n_seq_tiles, 0, i % n_seq_tiles))
    vt_shape = jax.ShapeDtypeStruct((t // seq, N_HEADS * LANES, seq), BF16)
    return pl.pallas_call(
        _proj_kernel,
        grid=(t // tm,),
        in_specs=[row(D_MODEL), full(wts["attn_norm"]), full(wts["w_in"]), full(wts["q_norm"]), full(wts["w_uq"]),
                  full(wts["kv_norm"]), full(wts["w_ukv"]), tab, tab, tab, tab],
        out_specs=[row(w) for w in outs] + [vt_spec],
        out_shape=[jax.ShapeDtypeStruct((t, w), F32 if j < 3 else BF16) for j, w in enumerate(outs)] + [vt_shape],
        compiler_params=pltpu.CompilerParams(dimension_semantics=("arbitrary",), vmem_limit_bytes=_vmem(56)),
        name="proj",
    )(x2d, wts["attn_norm"], wts["w_in"], wts["q_norm"], wts["w_uq"], wts["kv_norm"], wts["w_ukv"],
      cosa, sina, cosb, sinb)


def _attn_b_kernel(q_ref, k_ref, vt_ref, o_ref, st_ref, p_ref, *, seq):
    tq = q_ref.shape[1]
    chunks = [slice(i * ATTN_TK, (i + 1) * ATTN_TK) for i in range(seq // ATTN_TK)]
    halves = []
    for j in range(2):
        lanes = slice(LANES * j, LANES * (j + 1))
        q = q_ref[0, :, lanes]
        st_ref[j] = lax.dot_general(k_ref[0, :, lanes], q, (((1,), (1,)), ((), ())), preferred_element_type=F32)
        m = jnp.full((1, tq), NEG_INF, F32)
        for rows in chunks:
            m = jnp.maximum(m, jnp.max(st_ref[j, rows, :], axis=0, keepdims=True))
        for rows in chunks:
            p_ref[j, rows, :] = jnp.exp(st_ref[j, rows, :] - m).astype(BF16)
        acc = jnp.dot(vt_ref[0, lanes, :], p_ref[j], preferred_element_type=F32)
        halves.append((acc / acc[V_DIM:V_DIM + 1, :])[:V_DIM])
    o_ref[0] = jnp.concatenate(halves, axis=0).T.astype(BF16)


def _attn_b(qb, kb, vbt, batch, seq):
    q3 = qb.reshape(batch, seq, N_HEADS * LANES)
    k3 = kb.reshape(batch, seq, N_HEADS * LANES)
    tq = ATTN_TQ
    out = pl.pallas_call(
        functools.partial(_attn_b_kernel, seq=seq),
        grid=(batch, N_HEADS // 2, seq // tq),
        in_specs=[pl.BlockSpec((1, tq, 2 * LANES), lambda b, h, i: (b, i, h)),
                  pl.BlockSpec((1, seq, 2 * LANES), lambda b, h, i: (b, 0, h)),
                  pl.BlockSpec((1, 2 * LANES, seq), lambda b, h, i: (b, h, 0))],
        out_specs=pl.BlockSpec((1, tq, LANES), lambda b, h, i: (b, i, h)),
        out_shape=jax.ShapeDtypeStruct((batch, seq, N_HEADS * V_DIM), BF16),
        scratch_shapes=[pltpu.VMEM((2, seq, tq), F32), pltpu.VMEM((2, seq, tq), BF16)],
        compiler_params=pltpu.CompilerParams(dimension_semantics=("arbitrary",) * 3, vmem_limit_bytes=_vmem(48)),
        name="attn_b",
    )(q3, k3, vbt)
    return out.reshape(batch * seq, N_HEADS * V_DIM)


def _dil_all_kernel(bias_ref, q_ref, k_ref, v_ref, o_ref, acc_ref, m_ref, l_ref, *, seq):
    qb_rows = DIL_QB
    lane = lax.broadcasted_iota(jnp.int32, (qb_rows, LANES), 1)
    low = lane < HEAD_DIM
    for dil in DILATIONS:
        length = seq // dil
        win = min(2 * qb_rows, length)
        nblk = length // qb_rows

        def body(n, carry, dil=dil, length=length, win=win, nblk=nblk):
            res = n // nblk
            r0 = (n % nblk) * qb_rows
            start = jnp.clip(r0 - BAND, 0, length - win)
            if dil == 1:
                qrows = pl.ds(pl.multiple_of(r0, qb_rows), qb_rows)
                krows = pl.ds(pl.multiple_of(start, BAND), win)
            else:
                qrows = pl.ds(res + dil * r0, qb_rows, stride=dil)
                krows = pl.ds(res + dil * start, win, stride=dil)
            case = jnp.where(r0 == 0, 0, jnp.where(r0 == length - qb_rows, 2, 1))
            bias = bias_ref[case, :, :win]
            qp = q_ref[qrows, :]
            zero = jnp.zeros_like(qp)
            qq = jnp.concatenate([jnp.where(low, qp, zero), jnp.where(low, zero, qp)], axis=0).astype(BF16)
            kw = k_ref[krows, :].astype(BF16)
            vw = v_ref[krows, :].astype(BF16)
            s = lax.dot_general(qq, kw, (((1,), (1,)), ((), ())), preferred_element_type=F32) + bias
            m = jnp.max(s, axis=-1, keepdims=True)
            p = jnp.exp(s - m)
            l = jnp.sum(p, axis=-1, keepdims=True)
            pv = jnp.dot(p.astype(BF16), vw, preferred_element_type=F32)
            pick =lambda t: jnp.where(low, t[:qb_rows], t[qb_rows:])
            m_new, l_new, pv_new = pick(m), pick(l), pick(pv)
            if dil == DILATIONS[0]:
                m_ref[qrows, :] = m_new
                l_ref[qrows, :] = l_new
                acc_ref[qrows, :] = pv_new
            else:
                m_old = m_ref[qrows, :]
                top = jnp.maximum(m_old, m_new)
                a = jnp.exp(m_old - top)
                b = jnp.exp(m_new - top)
                m_ref[qrows, :] = top
                l_ref[qrows, :] = a * l_ref[qrows, :] + b * l_new
                acc_ref[qrows, :] = a * acc_ref[qrows, :] + b * pv_new
            return carry

        lax.fori_loop(0, dil * nblk, body, 0, unroll=DIL_UNROLL)
    for c in range(seq // 512):
        rows = slice(c * 512, (c + 1) * 512)
        o_ref[rows, :] = (acc_ref[rows, :] / l_ref[rows, :]).astype(BF16)


def _band_bias():
    q = jnp.arange(2 * DIL_QB, dtype=jnp.int32)[None, :, None] % DIL_QB
    k = jnp.arange(2 * DIL_QB, dtype=jnp.int32)[None, None, :]
    delta = jnp.array([0, -BAND, -DIL_QB], jnp.int32)[:, None, None]
    return jnp.where(jnp.abs(k + delta - q) <= BAND, 0.0, NEG_INF).astype(F32)


def _dilated_all(qa, ka, va, batch, seq):
    view = lambda a: a.reshape(batch, seq, A_WIDTH)
    spec = pl.BlockSpec((None, seq, LANES), lambda b, h: (b, 0, h))
    bias = _band_bias()
    out = pl.pallas_call(
        functools.partial(_dil_all_kernel, seq=seq),
        grid=(batch, A_WIDTH // LANES),
        in_specs=[pl.BlockSpec(bias.shape, lambda b, h: (0, 0, 0)), spec, spec, spec],
        out_specs=spec,
        out_shape=jax.ShapeDtypeStruct((batch, seq, A_WIDTH), BF16),
        scratch_shapes=[pltpu.VMEM((seq, LANES), F32)] * 3,
        compiler_params=pltpu.CompilerParams(dimension_semantics=("arbitrary",) * 2, vmem_limit_bytes=_vmem(48)),
        name="dilated",
    )(bias, view(qa), view(ka), view(va))
    return out.reshape(batch * seq, A_WIDTH)


def _post_kernel(xp_ref, oap_ref, obp_ref, xs_ref, oas_ref, obs_ref, wo_ref, g_ref, rwh_ref, rwl_ref, rb_ref,
                 x1_ref, hn_ref, route_ref, routet_ref, cout_ref, carry_ref, *, prompt_tiles):
    tm = xp_ref.shape[0]
    is_prompt = pl.program_id(0) < prompt_tiles

    @pl.when(pl.program_id(0) == 0)
    def _():
        carry_ref[...] = jnp.zeros_like(carry_ref)

    oa = jnp.where(is_prompt, oap_ref[...], oas_ref[...])
    ob = jnp.where(is_prompt, obp_ref[...], obs_ref[...])
    attn = jnp.dot(oa, wo_ref[:A_WIDTH, :], preferred_element_type=F32)
    attn += jnp.dot(ob, wo_ref[A_WIDTH:, :], preferred_element_type=F32)
    x1 = jnp.where(is_prompt, xp_ref[...], xs_ref[...]) + attn
    x1_ref[...] = x1
    hn = _rms(x1, g_ref[...])
    _store_token_tiles(hn_ref, hn)

    hi = hn.astype(BF16)
    lo = (hn - hi.astype(F32)).astype(BF16)
    logits = jnp.dot(hi, rwh_ref[...], preferred_element_type=F32)
    logits += jnp.dot(lo, rwh_ref[...], preferred_element_type=F32)
    logits += jnp.dot(hi, rwl_ref[...], preferred_element_type=F32)
    logits += rb_ref[...]

    lane = lax.broadcasted_iota(jnp.int32, (tm, LANES), 1)
    work = logits
    vals, sels = [], []
    for _ in range(TOP_K):
        mx = jnp.max(work, axis=-1, keepdims=True)
        first = jnp.min(jnp.where(work == mx, lane, LANES), axis=-1, keepdims=True)
        sel = lane == first
        work = jnp.where(sel, -jnp.inf, work)
        vals.append(mx)
        sels.append(sel)
    exps = [jnp.exp(v - vals[0]) for v in vals]
    den = exps[0] + exps[1] + exps[2] + exps[3]

    sel_all = (sels[0] | sels[1] | sels[2] | sels[3]).astype(F32)
    tri = (lax.broadcasted_iota(jnp.int32, (tm, tm), 1) < lax.broadcasted_iota(jnp.int32, (tm, tm), 0)).astype(BF16)
    before = jnp.dot(tri, sel_all.astype(BF16), preferred_element_type=F32) + carry_ref[0:1, :]
    carry_ref[0:1, :] = carry_ref[0:1, :] + jnp.sum(sel_all, axis=0, keepdims=True)
    cout_ref[...] = carry_ref[...]

    lane_f = lane.astype(F32)
    route = jnp.zeros((tm, LANES), F32)
    for k in range(TOP_K):
        idx = jnp.sum(jnp.where(sels[k], lane_f, 0.0), axis=-1, keepdims=True)
        rank = jnp.sum(jnp.where(sels[k], before, 0.0), axis=-1, keepdims=True)
        route = jnp.where(lane == k, idx, route)
        route = jnp.where(lane == TOP_K + k, rank, route)
        route = jnp.where(lane == 2 * TOP_K + k, exps[k] / den, route)
    route_ref[...] = route
    routet_ref[...] = route.T[:16, :]


def _post(prompt, sample, wts):
    tm = POST_TM
    n_p = prompt[0].shape[0] // tm
    n_s = sample[0].shape[0] // tm
    total = (n_p + n_s) * tm
    full = lambda a: pl.BlockSpec(a.shape, lambda i: (0,) * a.ndim)
    row_p = lambda w: pl.BlockSpec((tm, w), lambda i: (jnp.minimum(i, n_p - 1), 0))
    row_s = lambda w: pl.BlockSpec((tm, w), lambda i: (jnp.maximum(i - n_p, 0), 0))
    row = lambda w: pl.BlockSpec((tm, w), lambda i: (i, 0))
    widths = (D_MODEL, A_WIDTH, A_WIDTH)
    consts = [wts["w_o"], wts["ffn_norm"], wts["router_hi"], wts["router_lo"], wts["router_b"]]
    out_shape = [jax.ShapeDtypeStruct((total, D_MODEL), F32), jax.ShapeDtypeStruct((total * CHUNKS, LANES), F32),
                 jax.ShapeDtypeStruct((total, LANES), F32), jax.ShapeDtypeStruct((16, total), F32),
                 jax.ShapeDtypeStruct((8, LANES), F32)]
    out_specs = [row(D_MODEL), pl.BlockSpec((tm * CHUNKS, LANES), lambda i: (i, 0)), row(LANES),
                 pl.BlockSpec((16, tm), lambda i: (0, i)), pl.BlockSpec((8, LANES), lambda i: (0, 0))]
    return pl.pallas_call(
        functools.partial(_post_kernel, prompt_tiles=n_p),
        grid=(n_p + n_s,),
        in_specs=[row_p(w) for w in widths] + [row_s(w) for w in widths] + [full(c) for c in consts],
        out_specs=out_specs,
        out_shape=out_shape,
        scratch_shapes=[pltpu.VMEM((8, LANES), F32)],
        compiler_params=pltpu.CompilerParams(dimension_semantics=("arbitrary",), vmem_limit_bytes=_vmem(56)),
        name="post",
    )(*prompt, *sample, *consts)


def _dispatch_kernel(pos_ref, hn_ref, xs_ref, sem):
    tm = DISPATCH_TM

    def copy(t, slot):
        return pltpu.make_async_copy(hn_ref.at[t], xs_ref.at[slot], sem)

    def issue(t, c):
        for k in range(TOP_K):
            copy(t, pos_ref[0, 0, k * tm + t]).start(priority=k % 2)
        return c

    def drain(t, c):
        for k in range(TOP_K):
            copy(0, 0).wait()
        return c

    lax.fori_loop(0, tm, issue, 0, unroll=DMA_UNROLL)
    lax.fori_loop(0, tm, drain, 0, unroll=DMA_UNROLL)


def _dispatch(pos_blocks, hn, rows):
    tm = DISPATCH_TM
    hn3 = hn.reshape(-1, CHUNKS, LANES)
    return pl.pallas_call(
        _dispatch_kernel,
        grid=(hn3.shape[0] // tm,),
        in_specs=[pl.BlockSpec((1, 1, TOP_K * tm), lambda i: (i, 0, 0), memory_space=pltpu.SMEM),
                  pl.BlockSpec((tm, CHUNKS, LANES), lambda i: (i, 0, 0))],
        out_specs=pl.BlockSpec(memory_space=pl.ANY),
        out_shape=jax.ShapeDtypeStruct((rows, CHUNKS, LANES), F32),
        scratch_shapes=[pltpu.SemaphoreType.DMA(())],
        compiler_params=pltpu.CompilerParams(dimension_semantics=("arbitrary",)),
        name="dispatch",
    )(pos_blocks, hn3)


def _combine_kernel(pos_ref, pos_next_ref, ys_ref, x1_ref, route_ref, g_ref, o_ref, ybuf, sem, *, steps):
    tm = COMBINE_TM
    step = pl.program_id(0)
    cur = step % 2

    def copy(slot, buf, k, t):
        dst = ybuf.at[buf, k, pl.ds(pl.multiple_of(t * CHUNKS, CHUNKS), CHUNKS)]
        return pltpu.make_async_copy(ys_ref.at[slot], dst, sem.at[buf])

    def fetch(index_ref, buf):
        def issue(t, c):
            for k in range(TOP_K):
                copy(index_ref[0, 0, k * tm + t], buf, k, t).start(priority=k % 2)
            return c

        lax.fori_loop(0, tm, issue, 0, unroll=DMA_UNROLL)

    @pl.when(step == 0)
    def _():
        fetch(pos_ref, 0)

    @pl.when(step + 1 < steps)
    def _():
        fetch(pos_next_ref, 1 - cur)

    def drain(t, c):
        for k in range(TOP_K):
            copy(0, cur, k, 0).wait()
        return c

    lax.fori_loop(0, tm, drain, 0, unroll=DMA_UNROLL)
    route = route_ref[...]
    y = x1_ref[...]
    for k in range(TOP_K):
        y = y + route[:, 2 * TOP_K + k:2 * TOP_K + k + 1] * _load_token_tiles(ybuf.at[cur, k], tm)
    o_ref[...] = _rms(y, g_ref[...])


def _combine(pos_blocks, ys, x1, route, final_norm, row_off, rows):
    tm = COMBINE_TM
    off = row_off // tm
    last = off + rows // tm - 1
    pos_spec = lambda ahead: pl.BlockSpec((1, 1, TOP_K * tm), lambda i: (jnp.minimum(i + off + ahead, last), 0, 0),
                                          memory_space=pltpu.SMEM)
    return pl.pallas_call(
        functools.partial(_combine_kernel, steps=rows // tm),
        grid=(rows // tm,),
        in_specs=[pos_spec(0), pos_spec(1),
                  pl.BlockSpec(memory_space=pl.ANY),
                  pl.BlockSpec((tm, D_MODEL), lambda i: (i + off, 0)),
                  pl.BlockSpec((tm, LANES), lambda i: (i + off, 0)),
                  pl.BlockSpec((1, D_MODEL), lambda i: (0, 0))],
        out_specs=pl.BlockSpec((tm, D_MODEL), lambda i: (i, 0)),
        out_shape=jax.ShapeDtypeStruct((rows, D_MODEL), F32),
        scratch_shapes=[pltpu.VMEM((2, TOP_K, tm * CHUNKS, LANES), F32), pltpu.SemaphoreType.DMA((2,))],
        compiler_params=pltpu.CompilerParams(dimension_semantics=("arbitrary",), vmem_limit_bytes=_vmem(32)),
        name="combine",
    )(pos_blocks, pos_blocks, ys, x1, route, final_norm)


def _ffn_kernel(tile_ref, expert_ref, lo_ref, npairs_ref, x_ref, wgu_ref, bgu_ref, wd_ref, bd_ref, y_ref):
    step = pl.program_id(0)

    @pl.when(step < npairs_ref[0])
    def _():
        lo = lo_ref[step]
        n = FFN_TM // FFN_SUB
        for h in range(FFN_SUB):
            xb = _load_token_tiles(x_ref, n, h * n).astype(BF16)
            gate = jnp.dot(xb, wgu_ref[0, :, :D_FF], preferred_element_type=F32) + bgu_ref[0, :, :D_FF]
            up = jnp.dot(xb, wgu_ref[0, :, D_FF:], preferred_element_type=F32) + bgu_ref[0, :, D_FF:]
            gate = jnp.minimum(gate, SWIGLU_LIMIT)
            up = jnp.clip(up, -SWIGLU_LIMIT, SWIGLU_LIMIT)
            act = (up + 1.0) * gate * (1.0 / (1.0 + jnp.exp(-SWIGLU_ALPHA * gate)))
            res = jnp.dot(act.astype(BF16), wd_ref[0], preferred_element_type=F32) + bd_ref[0]
            row = h * n + lax.broadcasted_iota(jnp.int32, (n, 1), 0)
            _store_token_tiles(y_ref, jnp.where(row >= lo, res, _load_token_tiles(y_ref, n, h * n)), h * n)


def _ffn(pairs, xs, wts):
    tm = FFN_TM
    pair_tile, pair_expert, pair_lo, n_pairs = pairs
    x_map = lambda i, pt, pe, lo, n: (pt[i], 0)
    w_map = lambda i, pt, pe, lo, n: (pe[i], 0, 0)
    grid_spec = pltpu.PrefetchScalarGridSpec(
        num_scalar_prefetch=4,
        grid=(pair_tile.shape[0],),
        in_specs=[pl.BlockSpec((tm * CHUNKS, LANES), x_map),
                  pl.BlockSpec((1, D_MODEL, 2 * D_FF), w_map),
                  pl.BlockSpec((1, 1, 2 * D_FF), w_map),
                  pl.BlockSpec((1, D_FF, D_MODEL), w_map),
                  pl.BlockSpec((1, 1, D_MODEL), w_map)],
        out_specs=pl.BlockSpec((tm * CHUNKS, LANES), x_map),
    )
    ys = pl.pallas_call(
        _ffn_kernel,
        grid_spec=grid_spec,
        out_shape=jax.ShapeDtypeStruct((xs.shape[0] * CHUNKS, LANES), F32),
        compiler_params=pltpu.CompilerParams(dimension_semantics=("arbitrary",), vmem_limit_bytes=_vmem(56)),
        name="ffn",
    )(pair_tile, pair_expert, pair_lo, n_pairs, xs.reshape(-1, LANES), wts["w_gate_up"], wts["b_gate_up"],
      wts["w_down"], wts["b_down"])
    return ys.reshape(xs.shape)


def _rope_tables(seq):
    pos = jnp.arange(seq, dtype=F32)[:, None]

    def cs(dim):
        inv = 1.0 / (ROPE_THETA ** (jnp.arange(0, dim, 2, dtype=F32) / dim))
        ang = pos * inv[None, :]
        return jnp.cos(ang), jnp.sin(ang)

    ca, sa = cs(HEAD_DIM)
    cb, sb = cs(ROPE_DIM)
    one = jnp.ones((seq, NOPE_DIM), F32)
    pad1 = jnp.ones((seq, LANES - QK_DIM), F32)
    cosa = jnp.concatenate([ca, ca, ca, ca], axis=1)
    sina = jnp.concatenate([-sa, sa, -sa, sa], axis=1)
    cosb = jnp.concatenate([one, cb, cb, pad1], axis=1)
    sinb = jnp.concatenate([0.0 * one, -sb, sb, 0.0 * pad1], axis=1)
    return cosa, sina, cosb, sinb


def _prep_weights(attn_norm, w_in, q_norm, w_uq, kv_norm, w_ukv, w_o, ffn_norm, router_w, router_b,
                  w_gate_up, b_gate_up, w_down, b_down):
    w = w_in[0]
    zeros = lambda n: jnp.zeros((D_MODEL, n), F32)
    krope = jnp.concatenate([zeros(NOPE_DIM), w[:, 2176:], zeros(LANES - QK_DIM)], axis=1)
    w_in_p = jnp.concatenate([w[:, :2176], krope], axis=1).astype(BF16)
    uq = w_uq[0].reshape(Q_RANK, N_HEADS, QK_DIM)
    uq = jnp.pad(uq, ((0, 0), (0, 0), (0, LANES - QK_DIM))).reshape(Q_RANK, N_HEADS * LANES).astype(BF16)
    ukv = w_ukv[0].reshape(KV_RANK, N_HEADS, NOPE_DIM + V_DIM)
    pad = lambda a: jnp.pad(a, ((0, 0), (0, 0), (0, LANES - a.shape[2]))).reshape(KV_RANK, N_HEADS * LANES)
    ukv = jnp.concatenate([pad(ukv[:, :, :NOPE_DIM]), pad(ukv[:, :, NOPE_DIM:])], axis=1).astype(BF16)
    rw = jnp.pad(router_w[0], ((0, 0), (0, LANES - N_EXPERTS)))
    rw_hi = rw.astype(BF16)
    rw_lo = (rw - rw_hi.astype(F32)).astype(BF16)
    rb = jnp.concatenate([router_b[0], jnp.full((LANES - N_EXPERTS,), NEG_INF, F32)])[None, :]
    return {
        "attn_norm": attn_norm[0][None, :], "w_in": w_in_p, "q_norm": q_norm[0][None, :], "w_uq": uq,
        "kv_norm": kv_norm[0][None, :], "w_ukv": ukv, "w_o": w_o[0].astype(BF16), "ffn_norm": ffn_norm[0][None, :],
        "router_hi": rw_hi, "router_lo": rw_lo, "router_b": rb,
        "w_gate_up": w_gate_up[0].astype(BF16), "b_gate_up": b_gate_up[0][:, None, :],
        "w_down": w_down[0].astype(BF16), "b_down": b_down[0][:, None, :],
    }


def _routing(route_t, counts, total):
    i32 = jnp.int32
    idx = route_t[0:TOP_K].astype(i32)
    rank = route_t[TOP_K:2 * TOP_K].astype(i32)
    cnt = counts[0, :N_EXPERTS].astype(i32)
    seg_end = jnp.cumsum(cnt)
    seg_start = seg_end - cnt
    experts = jnp.arange(N_EXPERTS, dtype=i32)
    pos = rank + jnp.sum(jnp.where(idx[None] == experts[:, None, None], seg_start[:, None, None], 0), axis=0)

    first_tile = seg_start // FFN_TM
    n_per = jnp.where(cnt > 0, (seg_end - 1) // FFN_TM - first_tile + 1, 0)
    pair_end = jnp.cumsum(n_per)
    n_pairs = pair_end[-1]
    slots = total * TOP_K // FFN_TM + N_EXPERTS
    j = jnp.minimum(jnp.arange(slots, dtype=i32), n_pairs - 1)
    onehot = (jnp.sum((j[:, None] >= pair_end[None, :]).astype(i32), axis=1)[:, None] == experts[None, :]).astype(i32)
    pick = lambda table: jnp.sum(onehot * table[None, :], axis=1)
    pair_expert = pick(experts)
    pair_tile = pick(first_tile) + j - pick(pair_end - n_per)
    pair_lo = jnp.maximum(pick(seg_start) - pair_tile * FFN_TM, 0)

    def blocks(tm):
        return pos.reshape(TOP_K, total // tm, tm).transpose(1, 0, 2).reshape(total // tm, 1, TOP_K * tm)

    return blocks(DISPATCH_TM), blocks(COMBINE_TM), (pair_tile, pair_expert, pair_lo, n_pairs[None].astype(i32))


def _mixers(x, wts):
    batch, seq, _ = x.shape
    x2d = x.reshape(batch * seq, D_MODEL)
    qa, ka, va, qb, kb, vbt = _proj(x2d, seq, wts, _rope_tables(seq))
    oa = _dilated_all(qa, ka, va, batch, seq)
    ob = _attn_b(qb, kb, vbt, batch, seq)
    return x2d, oa, ob


def kernel(x_prompt, x_sample, attn_norm, w_in, q_norm, w_uq, kv_norm, w_ukv, w_o, ffn_norm, router_w, router_b,
           w_gate_up, b_gate_up, w_down, b_down, final_norm):
    wts = _prep_weights(attn_norm, w_in, q_norm, w_uq, kv_norm, w_ukv, w_o, ffn_norm, router_w, router_b,
                        w_gate_up, b_gate_up, w_down, b_down)
    sets = [x_prompt, x_sample]
    rows = [x.shape[0] * x.shape[1] for x in sets]
    total = sum(rows)
    x1, hn, route, route_t, counts = _post(_mixers(x_prompt, wts), _mixers(x_sample, wts), wts)
    pos_dispatch, pos_combine, pairs = _routing(route_t, counts, total)
    xs = _dispatch(pos_dispatch, hn, total * TOP_K)
    ys = _ffn(pairs, xs, wts)
    outs = []
    off = 0
    fnorm = final_norm[None, :]
    for x, n in zip(sets, rows):
        outs.append(_combine(pos_combine, ys, x1, route, fnorm, off, n).reshape(x.shape))
        off += n
    return tuple(outs)
```

```python
import functools

import jax
import jax.numpy as jnp
from jax import lax
from jax.experimental import pallas as pl
from jax.experimental.pallas import tpu as pltpu

D_MODEL = 1024
N_HEADS = 8
HEAD_DIM = 64
A_WIDTH = N_HEADS * HEAD_DIM
NOPE_DIM = 64
ROPE_DIM = 32
QK_DIM = NOPE_DIM + ROPE_DIM
V_DIM = 64
Q_RANK = 384
KV_RANK = 256
DILATIONS = (1, 4, 16)
BAND = 64
N_EXPERTS = 32
TOP_K = 4
D_FF = 1024
SWIGLU_LIMIT = 7.0
SWIGLU_ALPHA = 1.702
ROPE_THETA = 10000.0
NORM_EPS = 1e-5
NEG_INF = -1e30

LANES = 128
IN_SPLITS = (0, 512, 1024, 1536, 1920, 2176, 2304)

PROJ_TM = 512
ATTN_TQ = 512
ATTN_TK = 256
ATTN_SCORE_BYTES = 24 * 1024 * 1024
DIL_QB = 128
DIL_UNROLL = 8
POST_TM = 512
DISPATCH_TM = 256
DMA_UNROLL = 4
FFN_TM = 512
FFN_SUB = 2
COMBINE_TM = 256

F32 = jnp.float32
BF16 = jnp.bfloat16


def _vmem(mib):
    return mib * 1024 * 1024


def _rms(x, g):
    return x * lax.rsqrt(jnp.mean(x * x, axis=-1, keepdims=True) + NORM_EPS) * g


CHUNKS = D_MODEL // LANES


def _load_token_tiles(ref, n, first=0):
    rows = lambda c: pl.ds(first * CHUNKS + c, n, stride=CHUNKS)
    return jnp.concatenate([ref[rows(c), :] for c in range(CHUNKS)], axis=1)


def _store_token_tiles(ref, x, first=0):
    for c in range(CHUNKS):
        ref[pl.ds(first * CHUNKS + c, x.shape[0], stride=CHUNKS), :] = x[:, c * LANES:(c + 1) * LANES]


def _proj_kernel(x_ref, g_ref, win_ref, qn_ref, wuq_ref, kvn_ref, wukv_ref,
                 cosa_ref, sina_ref, cosb_ref, sinb_ref,
                 qa_ref, ka_ref, va_ref, qb_ref, kb_ref, vbt_ref):
    tm = x_ref.shape[0]
    hb = _rms(x_ref[...], g_ref[...]).astype(BF16)

    def mm(g):
        return jnp.dot(hb, win_ref[:, IN_SPLITS[g]:IN_SPLITS[g + 1]], preferred_element_type=F32)

    lane_a = lax.broadcasted_iota(jnp.int32, (tm, A_WIDTH), 1)
    first_half = (lane_a % HEAD_DIM) < (HEAD_DIM // 2)
    cosa = jnp.tile(cosa_ref[...], (1, A_WIDTH // LANES))
    sina = jnp.tile(sina_ref[...], (1, A_WIDTH // LANES))

    def rope_a(t):
        sw = jnp.where(first_half, pltpu.roll(t, A_WIDTH - HEAD_DIM // 2, 1), pltpu.roll(t, HEAD_DIM // 2, 1))
        return t * cosa + sw * sina

    qa_ref[...] = rope_a(mm(0)) * (HEAD_DIM ** -0.5)
    ka_ref[...] = rope_a(mm(1))
    va_ref[...] = mm(2)

    def rope_b(t, cosb, sinb):
        w = t.shape[1]
        lane = lax.broadcasted_iota(jnp.int32, t.shape, 1) % LANES
        half = ROPE_DIM // 2
        sw = jnp.where(lane < NOPE_DIM + half, pltpu.roll(t, w - half, 1), pltpu.roll(t, half, 1))
        return t * cosb + sw * sinb

    cq = _rms(mm(3), qn_ref[...]).astype(BF16)
    qb = jnp.dot(cq, wuq_ref[...], preferred_element_type=F32)
    cosb8 = jnp.tile(cosb_ref[...], (1, N_HEADS))
    sinb8 = jnp.tile(sinb_ref[...], (1, N_HEADS))
    qb_ref[...] = (rope_b(qb, cosb8, sinb8) * (QK_DIM ** -0.5)).astype(BF16)

    ckv = _rms(mm(4), kvn_ref[...]).astype(BF16)
    kv = jnp.dot(ckv, wukv_ref[...], preferred_element_type=F32)
    kpe = rope_b(mm(5), cosb_ref[...], sinb_ref[...])
    kb_ref[...] = (kv[:, :N_HEADS * LANES] + jnp.tile(kpe, (1, N_HEADS))).astype(BF16)
    lane_v = lax.broadcasted_iota(jnp.int32, (tm, N_HEADS * LANES), 1) % LANES
    vbt_ref[0] = jnp.where(lane_v == V_DIM, 1.0, kv[:, N_HEADS * LANES:]).T.astype(BF16)


def _proj(x2d, seq, wts, tables):
    t = x2d.shape[0]
    tm = PROJ_TM
    n_seq_tiles = seq // tm
    cosa, sina, cosb, sinb = tables
    full = lambda a: pl.BlockSpec(a.shape, lambda i: (0,) * a.ndim)
    tab = pl.BlockSpec((tm, LANES), lambda i: (i % n_seq_tiles, 0))
    row = lambda w: pl.BlockSpec((tm, w), lambda i: (i, 0))
    outs = [A_WIDTH, A_WIDTH, A_WIDTH, N_HEADS * LANES, N_HEADS * LANES]
    vt_spec = pl.BlockSpec((1, N_HEADS * LANES, tm), lambda i: (i // n_seq_tiles, 0, i % n_seq_tiles))
    vt_shape = jax.ShapeDtypeStruct((t // seq, N_HEADS * LANES, seq), BF16)
    return pl.pallas_call(
        _proj_kernel,
        grid=(t // tm,),
        in_specs=[row(D_MODEL), full(wts["attn_norm"]), full(wts["w_in"]), full(wts["q_norm"]), full(wts["w_uq"]),
                  full(wts["kv_norm"]), full(wts["w_ukv"]), tab, tab, tab, tab],
        out_specs=[row(w) for w in outs] + [vt_spec],
        out_shape=[jax.ShapeDtypeStruct((t, w), F32 if j < 3 else BF16) for j, w in enumerate(outs)] + [vt_shape],
        compiler_params=pltpu.CompilerParams(dimension_semantics=("arbitrary",), vmem_limit_bytes=_vmem(56)),
        name="proj",
    )(x2d, wts["attn_norm"], wts["w_in"], wts["q_norm"], wts["w_uq"], wts["kv_norm"], wts["w_ukv"],
      cosa, sina, cosb, sinb)


def _attn_b_kernel(q_ref, k_ref, vt_ref, o_ref, st_ref, p_ref, *, seq):
    tq = q_ref.shape[1]
    n_heads = st_ref.shape[0]
    chunks = [slice(i * ATTN_TK, (i + 1) * ATTN_TK) for i in range(seq // ATTN_TK)]
    outs = []
    for j in range(n_heads):
        lanes = slice(LANES * j, LANES * (j + 1))
        st_ref[j] = lax.dot_general(k_ref[0, :, lanes], q_ref[0, :, lanes], (((1,), (1,)), ((), ())),
                                    preferred_element_type=F32)
    for j in range(n_heads):
        lanes = slice(LANES * j, LANES * (j + 1))
        m = jnp.full((1, tq), NEG_INF, F32)
        for rows in chunks:
            m = jnp.maximum(m, jnp.max(st_ref[j, rows, :], axis=0, keepdims=True))
        for rows in chunks:
            p_ref[j, rows, :] = jnp.exp(st_ref[j, rows, :] - m).astype(BF16)
        acc = jnp.dot(vt_ref[0, lanes, :], p_ref[j], preferred_element_type=F32)
        outs.append((acc / acc[V_DIM:V_DIM + 1, :])[:V_DIM])
    o_ref[0] = jnp.concatenate(outs, axis=0).T.astype(BF16)


def _attn_b(qb, kb, vbt, batch, seq):
    q3 = qb.reshape(batch, seq, N_HEADS * LANES)
    k3 = kb.reshape(batch, seq, N_HEADS * LANES)
    tq = ATTN_TQ
    heads = max(2, min(N_HEADS, ATTN_SCORE_BYTES // (6 * seq * tq)))
    width = heads * LANES
    out = pl.pallas_call(
        functools.partial(_attn_b_kernel, seq=seq),
        grid=(batch, N_HEADS // heads, seq // tq),
        in_specs=[pl.BlockSpec((1, tq, width), lambda b, h, i: (b, i, h)),
                  pl.BlockSpec((1, seq, width), lambda b, h, i: (b, 0, h)),
                  pl.BlockSpec((1, width, seq), lambda b, h, i: (b, h, 0))],
        out_specs=pl.BlockSpec((1, tq, heads * V_DIM), lambda b, h, i: (b, i, h)),
        out_shape=jax.ShapeDtypeStruct((batch, seq, N_HEADS * V_DIM), BF16),
        scratch_shapes=[pltpu.VMEM((heads, seq, tq), F32), pltpu.VMEM((heads, seq, tq), BF16)],
        compiler_params=pltpu.CompilerParams(dimension_semantics=("arbitrary",) * 3, vmem_limit_bytes=_vmem(56)),
        name="attn_b",
    )(q3, k3, vbt)
    return out.reshape(batch * seq, N_HEADS * V_DIM)


def _dil_all_kernel(bias_ref, q_ref, k_ref, v_ref, o_ref, acc_ref, m_ref, l_ref, *, seq):
    qb_rows = DIL_QB
    lane = lax.broadcasted_iota(jnp.int32, (qb_rows, LANES), 1)
    low = lane < HEAD_DIM
    for dil in DILATIONS:
        length = seq // dil
        win = min(2 * qb_rows, length)
        nblk = length // qb_rows

        def body(n, carry, dil=dil, length=length, win=win, nblk=nblk):
            res = n // nblk
            r0 = (n % nblk) * qb_rows
            start = jnp.clip(r0 - BAND, 0, length - win)
            if dil == 1:
                qrows = pl.ds(pl.multiple_of(r0, qb_rows), qb_rows)
                krows = pl.ds(pl.multiple_of(start, BAND), win)
            else:
                qrows = pl.ds(res + dil * r0, qb_rows, stride=dil)
                krows = pl.ds(res + dil * start, win, stride=dil)
            case = jnp.where(r0 == 0, 0, jnp.where(r0 == length - qb_rows, 2, 1))
            bias = bias_ref[case, :, :win]
            qp = q_ref[qrows, :]
            zero = jnp.zeros_like(qp)
            qq = jnp.concatenate([jnp.where(low, qp, zero), jnp.where(low, zero, qp)], axis=0).astype(BF16)
            kw = k_ref[krows, :].astype(BF16)
            vw = v_ref[krows, :].astype(BF16)
            s = lax.dot_general(qq, kw, (((1,), (1,)), ((), ())), preferred_element_type=F32) + bias
            m = jnp.max(s, axis=-1, keepdims=True)
            p = jnp.exp(s - m)
            l = jnp.sum(p, axis=-1, keepdims=True)
            pv = jnp.dot(p.astype(BF16), vw, preferred_element_type=F32)
            pick =lambda t: jnp.where(low, t[:qb_rows], t[qb_rows:])
            m_new, l_new, pv_new = pick(m), pick(l), pick(pv)
            if dil == DILATIONS[0]:
                m_ref[qrows, :] = m_new
                l_ref[qrows, :] = l_new
                acc_ref[qrows, :] = pv_new
            else:
                m_old = m_ref[qrows, :]
                top = jnp.maximum(m_old, m_new)
                a = jnp.exp(m_old - top)
                b = jnp.exp(m_new - top)
                m_ref[qrows, :] = top
                l_ref[qrows, :] = a * l_ref[qrows, :] + b * l_new
                acc_ref[qrows, :] = a * acc_ref[qrows, :] + b * pv_new
            return carry

        lax.fori_loop(0, dil * nblk, body, 0, unroll=DIL_UNROLL)
    for c in range(seq // 512):
        rows = slice(c * 512, (c + 1) * 512)
        o_ref[rows, :] = (acc_ref[rows, :] / l_ref[rows, :]).astype(BF16)


def _band_bias():
    q = jnp.arange(2 * DIL_QB, dtype=jnp.int32)[None, :, None] % DIL_QB
    k = jnp.arange(2 * DIL_QB, dtype=jnp.int32)[None, None, :]
    delta = jnp.array([0, -BAND, -DIL_QB], jnp.int32)[:, None, None]
    return jnp.where(jnp.abs(k + delta - q) <= BAND, 0.0, NEG_INF).astype(F32)


def _dilated_all(qa, ka, va, batch, seq):
    view = lambda a: a.reshape(batch, seq, A_WIDTH)
    spec = pl.BlockSpec((None, seq, LANES), lambda b, h: (b, 0, h))
    bias = _band_bias()
    out = pl.pallas_call(
        functools.partial(_dil_all_kernel, seq=seq),
        grid=(batch, A_WIDTH // LANES),
        in_specs=[pl.BlockSpec(bias.shape, lambda b, h: (0, 0, 0)), spec, spec, spec],
        out_specs=spec,
        out_shape=jax.ShapeDtypeStruct((batch, seq, A_WIDTH), BF16),
        scratch_shapes=[pltpu.VMEM((seq, LANES), F32)] * 3,
        compiler_params=pltpu.CompilerParams(dimension_semantics=("arbitrary",) * 2, vmem_limit_bytes=_vmem(48)),
        name="dilated",
    )(bias, view(qa), view(ka), view(va))
    return out.reshape(batch * seq, A_WIDTH)


def _post_kernel(xp_ref, oap_ref, obp_ref, xs_ref, oas_ref, obs_ref, wo_ref, g_ref, rwh_ref, rwl_ref, rb_ref,
                 x1_ref, hn_ref, route_ref, routet_ref, cout_ref, carry_ref, *, prompt_tiles):
    tm = xp_ref.shape[0]
    is_prompt = pl.program_id(0) < prompt_tiles

    @pl.when(pl.program_id(0) == 0)
    def _():
        carry_ref[...] = jnp.zeros_like(carry_ref)

    oa = jnp.where(is_prompt, oap_ref[...], oas_ref[...])
    ob = jnp.where(is_prompt, obp_ref[...], obs_ref[...])
    attn = jnp.dot(oa, wo_ref[:A_WIDTH, :], preferred_element_type=F32)
    attn += jnp.dot(ob, wo_ref[A_WIDTH:, :], preferred_element_type=F32)
    x1 = jnp.where(is_prompt, xp_ref[...], xs_ref[...]) + attn
    x1_ref[...] = x1
    hn = _rms(x1, g_ref[...])
    _store_token_tiles(hn_ref, hn)

    hi = hn.astype(BF16)
    lo = (hn - hi.astype(F32)).astype(BF16)
    logits = jnp.dot(hi, rwh_ref[...], preferred_element_type=F32)
    logits += jnp.dot(lo, rwh_ref[...], preferred_element_type=F32)
    logits += jnp.dot(hi, rwl_ref[...], preferred_element_type=F32)
    logits += rb_ref[...]

    lane = lax.broadcasted_iota(jnp.int32, (tm, LANES), 1)
    work = logits
    vals, sels = [], []
    for _ in range(TOP_K):
        mx = jnp.max(work, axis=-1, keepdims=True)
        first = jnp.min(jnp.where(work == mx, lane, LANES), axis=-1, keepdims=True)
        sel = lane == first
        work = jnp.where(sel, -jnp.inf, work)
        vals.append(mx)
        sels.append(sel)
    exps = [jnp.exp(v - vals[0]) for v in vals]
    den = exps[0] + exps[1] + exps[2] + exps[3]

    sel_all = (sels[0] | sels[1] | sels[2] | sels[3]).astype(F32)
    tri = (lax.broadcasted_iota(jnp.int32, (tm, tm), 1) < lax.broadcasted_iota(jnp.int32, (tm, tm), 0)).astype(BF16)
    before = jnp.dot(tri, sel_all.astype(BF16), preferred_element_type=F32) + carry_ref[0:1, :]
    carry_ref[0:1, :] = carry_ref[0:1, :] + jnp.sum(sel_all, axis=0, keepdims=True)
    cout_ref[...] = carry_ref[...]

    lane_f = lane.astype(F32)
    route = jnp.zeros((tm, LANES), F32)
    for k in range(TOP_K):
        idx = jnp.sum(jnp.where(sels[k], lane_f, 0.0), axis=-1, keepdims=True)
        rank = jnp.sum(jnp.where(sels[k], before, 0.0), axis=-1, keepdims=True)
        route = jnp.where(lane == k, idx, route)
        route = jnp.where(lane == TOP_K + k, rank, route)
        route = jnp.where(lane == 2 * TOP_K + k, exps[k] / den, route)
    route_ref[...] = route
    routet_ref[...] = route.T[:16, :]


def _post(prompt, sample, wts):
    tm = POST_TM
    n_p = prompt[0].shape[0] // tm
    n_s = sample[0].shape[0] // tm
    total = (n_p + n_s) * tm
    full = lambda a: pl.BlockSpec(a.shape, lambda i: (0,) * a.ndim)
    row_p = lambda w: pl.BlockSpec((tm, w), lambda i: (jnp.minimum(i, n_p - 1), 0))
    row_s = lambda w: pl.BlockSpec((tm, w), lambda i: (jnp.maximum(i - n_p, 0), 0))
    row = lambda w: pl.BlockSpec((tm, w), lambda i: (i, 0))
    widths = (D_MODEL, A_WIDTH, A_WIDTH)
    consts = [wts["w_o"], wts["ffn_norm"], wts["router_hi"], wts["router_lo"], wts["router_b"]]
    out_shape = [jax.ShapeDtypeStruct((total, D_MODEL), F32), jax.ShapeDtypeStruct((total * CHUNKS, LANES), F32),
                 jax.ShapeDtypeStruct((total, LANES), F32), jax.ShapeDtypeStruct((16, total), F32),
                 jax.ShapeDtypeStruct((8, LANES), F32)]
    out_specs = [row(D_MODEL), pl.BlockSpec((tm * CHUNKS, LANES), lambda i: (i, 0)), row(LANES),
                 pl.BlockSpec((16, tm), lambda i: (0, i)), pl.BlockSpec((8, LANES), lambda i: (0, 0))]
    return pl.pallas_call(
        functools.partial(_post_kernel, prompt_tiles=n_p),
        grid=(n_p + n_s,),
        in_specs=[row_p(w) for w in widths] + [row_s(w) for w in widths] + [full(c) for c in consts],
        out_specs=out_specs,
        out_shape=out_shape,
        scratch_shapes=[pltpu.VMEM((8, LANES), F32)],
        compiler_params=pltpu.CompilerParams(dimension_semantics=("arbitrary",), vmem_limit_bytes=_vmem(56)),
        name="post",
    )(*prompt, *sample, *consts)


def _dispatch_kernel(pos_ref, hn_ref, xs_ref, sem):
    tm = DISPATCH_TM

    def copy(t, slot):
        return pltpu.make_async_copy(hn_ref.at[t], xs_ref.at[slot], sem)

    def issue(t, c):
        for k in range(TOP_K):
            copy(t, pos_ref[0, 0, k * tm + t]).start(priority=k % 2)
        return c

    def drain(t, c):
        for k in range(TOP_K):
            copy(0, 0).wait()
        return c

    lax.fori_loop(0, tm, issue, 0, unroll=DMA_UNROLL)
    lax.fori_loop(0, tm, drain, 0, unroll=DMA_UNROLL)


def _dispatch(pos_blocks, hn, rows):
    tm = DISPATCH_TM
    hn3 = hn.reshape(-1, CHUNKS, LANES)
    return pl.pallas_call(
        _dispatch_kernel,
        grid=(hn3.shape[0] // tm,),
        in_specs=[pl.BlockSpec((1, 1, TOP_K * tm), lambda i: (i, 0, 0), memory_space=pltpu.SMEM),
                  pl.BlockSpec((tm, CHUNKS, LANES), lambda i: (i, 0, 0))],
        out_specs=pl.BlockSpec(memory_space=pl.ANY),
        out_shape=jax.ShapeDtypeStruct((rows, CHUNKS, LANES), F32),
        scratch_shapes=[pltpu.SemaphoreType.DMA(())],
        compiler_params=pltpu.CompilerParams(dimension_semantics=("arbitrary",)),
        name="dispatch",
    )(pos_blocks, hn3)


def _combine_kernel(pos_ref, pos_next_ref, ys_ref, x1_ref, route_ref, g_ref, o_ref, ybuf, sem, *, steps):
    tm = COMBINE_TM
    step = pl.program_id(0)
    cur = step % 2

    def copy(slot, buf, k, t):
        dst = ybuf.at[buf, k, pl.ds(pl.multiple_of(t * CHUNKS, CHUNKS), CHUNKS)]
        return pltpu.make_async_copy(ys_ref.at[slot], dst, sem.at[buf])

    def fetch(index_ref, buf):
        def issue(t, c):
            for k in range(TOP_K):
                copy(index_ref[0, 0, k * tm + t], buf, k, t).start(priority=k % 2)
            return c

        lax.fori_loop(0, tm, issue, 0, unroll=DMA_UNROLL)

    @pl.when(step == 0)
    def _():
        fetch(pos_ref, 0)

    @pl.when(step + 1 < steps)
    def _():
        fetch(pos_next_ref, 1 - cur)

    def drain(t, c):
        for k in range(TOP_K):
            copy(0, cur, k, 0).wait()
        return c

    lax.fori_loop(0, tm, drain, 0, unroll=DMA_UNROLL)
    route = route_ref[...]
    y = x1_ref[...]
    for k in range(TOP_K):
        y = y + route[:, 2 * TOP_K + k:2 * TOP_K + k + 1] * _load_token_tiles(ybuf.at[cur, k], tm)
    o_ref[...] = _rms(y, g_ref[...])


def _combine(pos_blocks, ys, x1, route, final_norm, row_off, rows):
    tm = COMBINE_TM
    off = row_off // tm
    last = off + rows // tm - 1
    pos_spec = lambda ahead: pl.BlockSpec((1, 1, TOP_K * tm), lambda i: (jnp.minimum(i + off + ahead, last), 0, 0),
                                          memory_space=pltpu.SMEM)
    return pl.pallas_call(
        functools.partial(_combine_kernel, steps=rows // tm),
        grid=(rows // tm,),
        in_specs=[pos_spec(0), pos_spec(1),
                  pl.BlockSpec(memory_space=pl.ANY),
                  pl.BlockSpec((tm, D_MODEL), lambda i: (i + off, 0)),
                  pl.BlockSpec((tm, LANES), lambda i: (i + off, 0)),
                  pl.BlockSpec((1, D_MODEL), lambda i: (0, 0))],
        out_specs=pl.BlockSpec((tm, D_MODEL), lambda i: (i, 0)),
        out_shape=jax.ShapeDtypeStruct((rows, D_MODEL), F32),
        scratch_shapes=[pltpu.VMEM((2, TOP_K, tm * CHUNKS, LANES), F32), pltpu.SemaphoreType.DMA((2,))],
        compiler_params=pltpu.CompilerParams(dimension_semantics=("arbitrary",), vmem_limit_bytes=_vmem(32)),
        name="combine",
    )(pos_blocks, pos_blocks, ys, x1, route, final_norm)


def _ffn_kernel(tile_ref, expert_ref, npairs_ref, x_ref, wgu_ref, bgu_ref, wd_ref, bd_ref, y_ref):
    step = pl.program_id(0)
    n = FFN_TM // FFN_SUB

    @pl.when(step >= npairs_ref[0])
    def _():
        y_ref[...] = jnp.zeros_like(y_ref)

    @pl.when(step < npairs_ref[0])
    def _():
        for h in range(FFN_SUB):
            xb = _load_token_tiles(x_ref, n, h * n).astype(BF16)
            gate = jnp.dot(xb, wgu_ref[0, :, :D_FF], preferred_element_type=F32) + bgu_ref[0, :, :D_FF]
            up = jnp.dot(xb, wgu_ref[0, :, D_FF:], preferred_element_type=F32) + bgu_ref[0, :, D_FF:]
            gate = jnp.minimum(gate, SWIGLU_LIMIT)
            up = jnp.clip(up, -SWIGLU_LIMIT, SWIGLU_LIMIT)
            act = (up + 1.0) * gate * (1.0 / (1.0 + jnp.exp(-SWIGLU_ALPHA * gate)))
            res = jnp.dot(act.astype(BF16), wd_ref[0], preferred_element_type=F32) + bd_ref[0]
            _store_token_tiles(y_ref, res, h * n)


def _ffn(pairs, xs, wts):
    tm = FFN_TM
    pair_tile, pair_expert, n_pairs = pairs
    slots = pair_tile.shape[0]
    x_map = lambda i, pt, pe, n: (pt[i], 0)
    w_map = lambda i, pt, pe, n: (pe[i], 0, 0)
    grid_spec = pltpu.PrefetchScalarGridSpec(
        num_scalar_prefetch=3,
        grid=(slots,),
        in_specs=[pl.BlockSpec((tm * CHUNKS, LANES), x_map),
                  pl.BlockSpec((1, D_MODEL, 2 * D_FF), w_map),
                  pl.BlockSpec((1, 1, 2 * D_FF), w_map),
                  pl.BlockSpec((1, D_FF, D_MODEL), w_map),
                  pl.BlockSpec((1, 1, D_MODEL), w_map)],
        out_specs=pl.BlockSpec((tm * CHUNKS, LANES), lambda i, pt, pe, n: (i, 0)),
    )
    ys = pl.pallas_call(
        _ffn_kernel,
        grid_spec=grid_spec,
        out_shape=jax.ShapeDtypeStruct((slots * tm * CHUNKS, LANES), F32),
        compiler_params=pltpu.CompilerParams(dimension_semantics=("arbitrary",), vmem_limit_bytes=_vmem(56)),
        name="ffn",
    )(pair_tile, pair_expert, n_pairs, xs.reshape(-1, LANES), wts["w_gate_up"], wts["b_gate_up"],
      wts["w_down"], wts["b_down"])
    return ys.reshape(slots * tm, CHUNKS, LANES)


def _rope_tables(seq):
    pos = jnp.arange(seq, dtype=F32)[:, None]

    def cs(dim):
        inv = 1.0 / (ROPE_THETA ** (jnp.arange(0, dim, 2, dtype=F32) / dim))
        ang = pos * inv[None, :]
        return jnp.cos(ang), jnp.sin(ang)

    ca, sa = cs(HEAD_DIM)
    cb, sb = cs(ROPE_DIM)
    one = jnp.ones((seq, NOPE_DIM), F32)
    pad1 = jnp.ones((seq, LANES - QK_DIM), F32)
    cosa = jnp.concatenate([ca, ca, ca, ca], axis=1)
    sina = jnp.concatenate([-sa, sa, -sa, sa], axis=1)
    cosb = jnp.concatenate([one, cb, cb, pad1], axis=1)
    sinb = jnp.concatenate([0.0 * one, -sb, sb, 0.0 * pad1], axis=1)
    return cosa, sina, cosb, sinb


def _prep_weights(attn_norm, w_in, q_norm, w_uq, kv_norm, w_ukv, w_o, ffn_norm, router_w, router_b,
                  w_gate_up, b_gate_up, w_down, b_down):
    w = w_in[0]
    zeros = lambda n: jnp.zeros((D_MODEL, n), F32)
    krope = jnp.concatenate([zeros(NOPE_DIM), w[:, 2176:], zeros(LANES - QK_DIM)], axis=1)
    w_in_p = jnp.concatenate([w[:, :2176], krope], axis=1).astype(BF16)
    uq = w_uq[0].reshape(Q_RANK, N_HEADS, QK_DIM)
    uq = jnp.pad(uq, ((0, 0), (0, 0), (0, LANES - QK_DIM))).reshape(Q_RANK, N_HEADS * LANES).astype(BF16)
    ukv = w_ukv[0].reshape(KV_RANK, N_HEADS, NOPE_DIM + V_DIM)
    pad = lambda a: jnp.pad(a, ((0, 0), (0, 0), (0, LANES - a.shape[2]))).reshape(KV_RANK, N_HEADS * LANES)
    ukv = jnp.concatenate([pad(ukv[:, :, :NOPE_DIM]), pad(ukv[:, :, NOPE_DIM:])], axis=1).astype(BF16)
    rw = jnp.pad(router_w[0], ((0, 0), (0, LANES - N_EXPERTS)))
    rw_hi = rw.astype(BF16)
    rw_lo = (rw - rw_hi.astype(F32)).astype(BF16)
    rb = jnp.concatenate([router_b[0], jnp.full((LANES - N_EXPERTS,), NEG_INF, F32)])[None, :]
    return {
        "attn_norm": attn_norm[0][None, :], "w_in": w_in_p, "q_norm": q_norm[0][None, :], "w_uq": uq,
        "kv_norm": kv_norm[0][None, :], "w_ukv": ukv, "w_o": w_o[0].astype(BF16), "ffn_norm": ffn_norm[0][None, :],
        "router_hi": rw_hi, "router_lo": rw_lo, "router_b": rb,
        "w_gate_up": w_gate_up[0].astype(BF16), "b_gate_up": b_gate_up[0][:, None, :],
        "w_down": w_down[0].astype(BF16), "b_down": b_down[0][:, None, :],
    }


def _routing(route_t, counts, total):
    i32 = jnp.int32
    idx = route_t[0:TOP_K].astype(i32)
    rank = route_t[TOP_K:2 * TOP_K].astype(i32)
    cnt = counts[0, :N_EXPERTS].astype(i32)
    seg_end = jnp.cumsum(cnt)
    seg_start = seg_end - cnt
    experts = jnp.arange(N_EXPERTS, dtype=i32)
    per_expert = lambda table: jnp.sum(jnp.where(idx[None] == experts[:, None, None], table[:, None, None], 0), axis=0)
    pos = rank + per_expert(seg_start)

    first_tile = seg_start // FFN_TM
    n_per = jnp.where(cnt > 0, (seg_end - 1) // FFN_TM - first_tile + 1, 0)
    pair_end = jnp.cumsum(n_per)
    n_pairs = pair_end[-1]
    pos_out = pos + per_expert((pair_end - n_per - first_tile) * FFN_TM)
    slots = total * TOP_K // FFN_TM + N_EXPERTS
    j = jnp.minimum(jnp.arange(slots, dtype=i32), n_pairs - 1)
    onehot = (jnp.sum((j[:, None] >= pair_end[None, :]).astype(i32), axis=1)[:, None] == experts[None, :]).astype(i32)
    pick = lambda table: jnp.sum(onehot * table[None, :], axis=1)
    pair_expert = pick(experts)
    pair_tile = pick(first_tile) + j - pick(pair_end - n_per)

    def blocks(p, tm):
        return p.reshape(TOP_K, total // tm, tm).transpose(1, 0, 2).reshape(total // tm, 1, TOP_K * tm)

    return blocks(pos, DISPATCH_TM), blocks(pos_out, COMBINE_TM), (pair_tile, pair_expert, n_pairs[None].astype(i32))


def _mixers(x, wts):
    batch, seq, _ = x.shape
    x2d = x.reshape(batch * seq, D_MODEL)
    qa, ka, va, qb, kb, vbt = _proj(x2d, seq, wts, _rope_tables(seq))
    oa = _dilated_all(qa, ka, va, batch, seq)
    ob = _attn_b(qb, kb, vbt, batch, seq)
    return x2d, oa, ob


def kernel(x_prompt, x_sample, attn_norm, w_in, q_norm, w_uq, kv_norm, w_ukv, w_o, ffn_norm, router_w, router_b,
           w_gate_up, b_gate_up, w_down, b_down, final_norm):
    wts = _prep_weights(attn_norm, w_in, q_norm, w_uq, kv_norm, w_ukv, w_o, ffn_norm, router_w, router_b,
                        w_gate_up, b_gate_up, w_down, b_down)
    sets = [x_prompt, x_sample]
    rows = [x.shape[0] * x.shape[1] for x in sets]
    total = sum(rows)
    x1, hn, route, route_t, counts = _post(_mixers(x_prompt, wts), _mixers(x_sample, wts), wts)
    pos_dispatch, pos_combine, pairs = _routing(route_t, counts, total)
    xs = _dispatch(pos_dispatch, hn, total * TOP_K)
    ys = _ffn(pairs, xs, wts)
    outs = []
    off = 0
    fnorm = final_norm[None, :]
    for x, n in zip(sets, rows):
        outs.append(_combine(pos_combine, ys, x1, route, fnorm, off, n).reshape(x.shape))
        off += n
    return tuple(outs)
```

```python
import functools

import jax
import jax.numpy as jnp
from jax import lax
from jax.experimental import pallas as pl
from jax.experimental.pallas import tpu as pltpu

D_MODEL = 1024
N_HEADS = 8
HEAD_DIM = 64
A_WIDTH = N_HEADS * HEAD_DIM
NOPE_DIM = 64
ROPE_DIM = 32
QK_DIM = NOPE_DIM + ROPE_DIM
V_DIM = 64
Q_RANK = 384
KV_RANK = 256
DILATIONS = (1, 4, 16)
BAND = 64
N_EXPERTS = 32
TOP_K = 4
D_FF = 1024
SWIGLU_LIMIT = 7.0
SWIGLU_ALPHA = 1.702
ROPE_THETA = 10000.0
NORM_EPS = 1e-5
NEG_INF = -1e30

LANES = 128
IN_SPLITS = (0, 512, 1024, 1536, 1920, 2176, 2304)

PROJ_TM = 512
ATTN_TQ = 512
ATTN_TK = 256
ATTN_SCORE_BYTES = 24 * 1024 * 1024
DIL_QB = 128
DIL_UNROLL = 16
POST_TM = 512
DISPATCH_TM = 256
DMA_UNROLL = 4
FFN_TM = 512
FFN_SUB = 2
FFN_CAST_ROWS = 256
COMBINE_TM = 256

F32 = jnp.float32
BF16 = jnp.bfloat16


def _vmem(mib):
    return mib * 1024 * 1024


def _rms(x, g):
    return x * lax.rsqrt(jnp.mean(x * x, axis=-1, keepdims=True) + NORM_EPS) * g


CHUNKS = D_MODEL // LANES


def _load_token_tiles(ref, n, first=0):
    rows = lambda c: pl.ds(first * CHUNKS + c, n, stride=CHUNKS)
    return jnp.concatenate([ref[rows(c), :] for c in range(CHUNKS)], axis=1)


def _store_token_tiles(ref, x, first=0):
    for c in range(CHUNKS):
        ref[pl.ds(first * CHUNKS + c, x.shape[0], stride=CHUNKS), :] = x[:, c * LANES:(c + 1) * LANES]


def _proj_kernel(x_ref, g_ref, win_ref, qn_ref, wuq_ref, kvn_ref, wukv_ref,
                 cosa_ref, sina_ref, cosb_ref, sinb_ref,
                 qa_ref, ka_ref, va_ref, qb_ref, kb_ref, vbt_ref):
    tm = x_ref.shape[0]
    hb = _rms(x_ref[...], g_ref[...]).astype(BF16)

    def mm(g):
        return jnp.dot(hb, win_ref[:, IN_SPLITS[g]:IN_SPLITS[g + 1]], preferred_element_type=F32)

    lane_a = lax.broadcasted_iota(jnp.int32, (tm, A_WIDTH), 1)
    first_half = (lane_a % HEAD_DIM) < (HEAD_DIM // 2)
    cosa = jnp.tile(cosa_ref[...], (1, A_WIDTH // LANES))
    sina = jnp.tile(sina_ref[...], (1, A_WIDTH // LANES))

    def rope_a(t):
        sw = jnp.where(first_half, pltpu.roll(t, A_WIDTH - HEAD_DIM // 2, 1), pltpu.roll(t, HEAD_DIM // 2, 1))
        return t * cosa + sw * sina

    qa_ref[...] = rope_a(mm(0)) * (HEAD_DIM ** -0.5)
    ka_ref[...] = rope_a(mm(1))
    va_ref[...] = mm(2)

    def rope_b(t, cosb, sinb):
        w = t.shape[1]
        lane = lax.broadcasted_iota(jnp.int32, t.shape, 1) % LANES
        half = ROPE_DIM // 2
        sw = jnp.where(lane < NOPE_DIM + half, pltpu.roll(t, w - half, 1), pltpu.roll(t, half, 1))
        return t * cosb + sw * sinb

    cq = _rms(mm(3), qn_ref[...]).astype(BF16)
    qb = jnp.dot(cq, wuq_ref[...], preferred_element_type=F32)
    cosb8 = jnp.tile(cosb_ref[...], (1, N_HEADS))
    sinb8 = jnp.tile(sinb_ref[...], (1, N_HEADS))
    qb_ref[...] = (rope_b(qb, cosb8, sinb8) * (QK_DIM ** -0.5)).astype(BF16)

    ckv = _rms(mm(4), kvn_ref[...]).astype(BF16)
    kv = jnp.dot(ckv, wukv_ref[...], preferred_element_type=F32)
    kpe = rope_b(mm(5), cosb_ref[...], sinb_ref[...])
    kb_ref[...] = (kv[:, :N_HEADS * LANES] + jnp.tile(kpe, (1, N_HEADS))).astype(BF16)
    lane_v = lax.broadcasted_iota(jnp.int32, (tm, N_HEADS * LANES), 1) % LANES
    vbt_ref[0] = jnp.where(lane_v == V_DIM, 1.0, kv[:, N_HEADS * LANES:]).T.astype(BF16)


def _proj(x2d, seq, wts, tables):
    t = x2d.shape[0]
    tm = PROJ_TM
    n_seq_tiles = seq // tm
    cosa, sina, cosb, sinb = tables
    full = lambda a: pl.BlockSpec(a.shape, lambda i: (0,) * a.ndim)
    tab = pl.BlockSpec((tm, LANES), lambda i: (i % n_seq_tiles, 0))
    row = lambda w: pl.BlockSpec((tm, w), lambda i: (i, 0))
    outs = [A_WIDTH, A_WIDTH, A_WIDTH, N_HEADS * LANES, N_HEADS * LANES]
    vt_spec = pl.BlockSpec((1, N_HEADS * LANES, tm), lambda i: (i // n_seq_tiles, 0, i % n_seq_tiles))
    vt_shape = jax.ShapeDtypeStruct((t // seq, N_HEADS * LANES, seq), BF16)
    return pl.pallas_call(
        _proj_kernel,
        grid=(t // tm,),
        in_specs=[row(D_MODEL), full(wts["attn_norm"]), full(wts["w_in"]), full(wts["q_norm"]), full(wts["w_uq"]),
                  full(wts["kv_norm"]), full(wts["w_ukv"]), tab, tab, tab, tab],
        out_specs=[row(w) for w in outs] + [vt_spec],
        out_shape=[jax.ShapeDtypeStruct((t, w), F32 if j < 3 else BF16) for j, w in enumerate(outs)] + [vt_shape],
        compiler_params=pltpu.CompilerParams(dimension_semantics=("arbitrary",), vmem_limit_bytes=_vmem(56)),
        name="proj",
    )(x2d, wts["attn_norm"], wts["w_in"], wts["q_norm"], wts["w_uq"], wts["kv_norm"], wts["w_ukv"],
      cosa, sina, cosb, sinb)


def _attn_b_kernel(q_ref, k_ref, vt_ref, o_ref, st_ref, p_ref, *, seq):
    tq = q_ref.shape[1]
    n_heads = st_ref.shape[0]
    chunks = [slice(i * ATTN_TK, (i + 1) * ATTN_TK) for i in range(seq // ATTN_TK)]
    outs = []
    for j in range(n_heads):
        lanes = slice(LANES * j, LANES * (j + 1))
        st_ref[j] = lax.dot_general(k_ref[0, :, lanes], q_ref[0, :, lanes], (((1,), (1,)), ((), ())),
                                    preferred_element_type=F32)
    for j in range(n_heads):
        lanes = slice(LANES * j, LANES * (j + 1))
        m = jnp.full((1, tq), NEG_INF, F32)
        for rows in chunks:
            m = jnp.maximum(m, jnp.max(st_ref[j, rows, :], axis=0, keepdims=True))
        for rows in chunks:
            p_ref[j, rows, :] = jnp.exp(st_ref[j, rows, :] - m).astype(BF16)
        acc = jnp.dot(vt_ref[0, lanes, :], p_ref[j], preferred_element_type=F32)
        outs.append((acc / acc[V_DIM:V_DIM + 1, :])[:V_DIM])
    o_ref[0] = jnp.concatenate(outs, axis=0).T.astype(BF16)


def _attn_b(qb, kb, vbt, batch, seq):
    q3 = qb.reshape(batch, seq, N_HEADS * LANES)
    k3 = kb.reshape(batch, seq, N_HEADS * LANES)
    tq = ATTN_TQ
    heads = max(2, min(N_HEADS, ATTN_SCORE_BYTES // (6 * seq * tq)))
    width = heads * LANES
    out = pl.pallas_call(
        functools.partial(_attn_b_kernel, seq=seq),
        grid=(batch, N_HEADS // heads, seq // tq),
        in_specs=[pl.BlockSpec((1, tq, width), lambda b, h, i: (b, i, h)),
                  pl.BlockSpec((1, seq, width), lambda b, h, i: (b, 0, h)),
                  pl.BlockSpec((1, width, seq), lambda b, h, i: (b, h, 0))],
        out_specs=pl.BlockSpec((1, tq, heads * V_DIM), lambda b, h, i: (b, i, h)),
        out_shape=jax.ShapeDtypeStruct((batch, seq, N_HEADS * V_DIM), BF16),
        scratch_shapes=[pltpu.VMEM((heads, seq, tq), F32), pltpu.VMEM((heads, seq, tq), BF16)],
        compiler_params=pltpu.CompilerParams(dimension_semantics=("arbitrary",) * 3, vmem_limit_bytes=_vmem(56)),
        name="attn_b",
    )(q3, k3, vbt)
    return out.reshape(batch * seq, N_HEADS * V_DIM)


def _dil_all_kernel(bias_ref, q_ref, k_ref, v_ref, o_ref, acc_ref, m_ref, l_ref, *, seq):
    qb_rows = DIL_QB
    lane = lax.broadcasted_iota(jnp.int32, (qb_rows, LANES), 1)
    low = lane < HEAD_DIM
    for dil in DILATIONS:
        length = seq // dil
        win = min(2 * qb_rows, length)
        nblk = length // qb_rows

        def body(n, carry, dil=dil, length=length, win=win, nblk=nblk):
            res = n // nblk
            r0 = (n % nblk) * qb_rows
            start = jnp.clip(r0 - BAND, 0, length - win)
            if dil == 1:
                qrows = pl.ds(pl.multiple_of(r0, qb_rows), qb_rows)
                krows = pl.ds(pl.multiple_of(start, BAND), win)
            else:
                qrows = pl.ds(res + dil * r0, qb_rows, stride=dil)
                krows = pl.ds(res + dil * start, win, stride=dil)
            case = jnp.where(r0 == 0, 0, jnp.where(r0 == length - qb_rows, 2, 1))
            bias = bias_ref[case, :, :win]
            qp = q_ref[qrows, :]
            zero = jnp.zeros_like(qp)
            qq = jnp.concatenate([jnp.where(low, qp, zero), jnp.where(low, zero, qp)], axis=0).astype(BF16)
            kw = k_ref[krows, :].astype(BF16)
            vw = v_ref[krows, :].astype(BF16)
            s = lax.dot_general(qq, kw, (((1,), (1,)), ((), ())), preferred_element_type=F32) + bias
            m = jnp.max(s, axis=-1, keepdims=True)
            p = jnp.exp(s - m)
            l = jnp.sum(p, axis=-1, keepdims=True)
            pv = jnp.dot(p.astype(BF16), vw, preferred_element_type=F32)
            pick =lambda t: jnp.where(low, t[:qb_rows], t[qb_rows:])
            m_new, l_new, pv_new = pick(m), pick(l), pick(pv)
            if dil == DILATIONS[0]:
                m_ref[qrows, :] = m_new
                l_ref[qrows, :] = l_new
                acc_ref[qrows, :] = pv_new
            else:
                m_old = m_ref[qrows, :]
                top = jnp.maximum(m_old, m_new)
                a = jnp.exp(m_old - top)
                b = jnp.exp(m_new - top)
                m_ref[qrows, :] = top
                l_ref[qrows, :] = a * l_ref[qrows, :] + b * l_new
                acc_ref[qrows, :] = a * acc_ref[qrows, :] + b * pv_new
            return carry

        lax.fori_loop(0, dil * nblk, body, 0, unroll=DIL_UNROLL)
    for c in range(seq // 512):
        rows = slice(c * 512, (c + 1) * 512)
        o_ref[rows, :] = (acc_ref[rows, :] / l_ref[rows, :]).astype(BF16)


def _band_bias():
    q = jnp.arange(2 * DIL_QB, dtype=jnp.int32)[None, :, None] % DIL_QB
    k = jnp.arange(2 * DIL_QB, dtype=jnp.int32)[None, None, :]
    delta = jnp.array([0, -BAND, -DIL_QB], jnp.int32)[:, None, None]
    return jnp.where(jnp.abs(k + delta - q) <= BAND, 0.0, NEG_INF).astype(F32)


def _dilated_all(qa, ka, va, batch, seq):
    view = lambda a: a.reshape(batch, seq, A_WIDTH)
    spec = pl.BlockSpec((None, seq, LANES), lambda b, h: (b, 0, h))
    bias = _band_bias()
    out = pl.pallas_call(
        functools.partial(_dil_all_kernel, seq=seq),
        grid=(batch, A_WIDTH // LANES),
        in_specs=[pl.BlockSpec(bias.shape, lambda b, h: (0, 0, 0)), spec, spec, spec],
        out_specs=spec,
        out_shape=jax.ShapeDtypeStruct((batch, seq, A_WIDTH), BF16),
        scratch_shapes=[pltpu.VMEM((seq, LANES), F32)] * 3,
        compiler_params=pltpu.CompilerParams(dimension_semantics=("arbitrary",) * 2, vmem_limit_bytes=_vmem(48)),
        name="dilated",
    )(bias, view(qa), view(ka), view(va))
    return out.reshape(batch * seq, A_WIDTH)


def _post_kernel(xp_ref, oap_ref, obp_ref, xs_ref, oas_ref, obs_ref, wo_ref, g_ref, rwh_ref, rwl_ref, rb_ref,
                 x1_ref, hn_ref, route_ref, routet_ref, cout_ref, carry_ref, *, prompt_tiles):
    tm = xp_ref.shape[0]
    is_prompt = pl.program_id(0) < prompt_tiles

    @pl.when(pl.program_id(0) == 0)
    def _():
        carry_ref[...] = jnp.zeros_like(carry_ref)

    oa = jnp.where(is_prompt, oap_ref[...], oas_ref[...])
    ob = jnp.where(is_prompt, obp_ref[...], obs_ref[...])
    attn = jnp.dot(oa, wo_ref[:A_WIDTH, :], preferred_element_type=F32)
    attn += jnp.dot(ob, wo_ref[A_WIDTH:, :], preferred_element_type=F32)
    x1 = jnp.where(is_prompt, xp_ref[...], xs_ref[...]) + attn
    x1_ref[...] = x1
    hn = _rms(x1, g_ref[...])
    _store_token_tiles(hn_ref, hn)

    hi = hn.astype(BF16)
    lo = (hn - hi.astype(F32)).astype(BF16)
    logits = jnp.dot(hi, rwh_ref[...], preferred_element_type=F32)
    logits += jnp.dot(lo, rwh_ref[...], preferred_element_type=F32)
    logits += jnp.dot(hi, rwl_ref[...], preferred_element_type=F32)
    logits += rb_ref[...]

    lane = lax.broadcasted_iota(jnp.int32, (tm, LANES), 1)
    work = logits
    vals, sels = [], []
    for _ in range(TOP_K):
        mx = jnp.max(work, axis=-1, keepdims=True)
        first = jnp.min(jnp.where(work == mx, lane, LANES), axis=-1, keepdims=True)
        sel = lane == first
        work = jnp.where(sel, -jnp.inf, work)
        vals.append(mx)
        sels.append(sel)
    exps = [jnp.exp(v - vals[0]) for v in vals]
    den = exps[0] + exps[1] + exps[2] + exps[3]

    sel_all = (sels[0] | sels[1] | sels[2] | sels[3]).astype(F32)
    tri = (lax.broadcasted_iota(jnp.int32, (tm, tm), 1) < lax.broadcasted_iota(jnp.int32, (tm, tm), 0)).astype(BF16)
    before = jnp.dot(tri, sel_all.astype(BF16), preferred_element_type=F32) + carry_ref[0:1, :]
    carry_ref[0:1, :] = carry_ref[0:1, :] + jnp.sum(sel_all, axis=0, keepdims=True)
    cout_ref[...] = carry_ref[...]

    lane_f = lane.astype(F32)
    route = jnp.zeros((tm, LANES), F32)
    for k in range(TOP_K):
        idx = jnp.sum(jnp.where(sels[k], lane_f, 0.0), axis=-1, keepdims=True)
        rank = jnp.sum(jnp.where(sels[k], before, 0.0), axis=-1, keepdims=True)
        route = jnp.where(lane == k, idx, route)
        route = jnp.where(lane == TOP_K + k, rank, route)
        route = jnp.where(lane == 2 * TOP_K + k, exps[k] / den, route)
    route_ref[...] = route
    routet_ref[...] = route.T[:16, :]


def _post(prompt, sample, wts):
    tm = POST_TM
    n_p = prompt[0].shape[0] // tm
    n_s = sample[0].shape[0] // tm
    total = (n_p + n_s) * tm
    full = lambda a: pl.BlockSpec(a.shape, lambda i: (0,) * a.ndim)
    row_p = lambda w: pl.BlockSpec((tm, w), lambda i: (jnp.minimum(i, n_p - 1), 0))
    row_s = lambda w: pl.BlockSpec((tm, w), lambda i: (jnp.maximum(i - n_p, 0), 0))
    row = lambda w: pl.BlockSpec((tm, w), lambda i: (i, 0))
    widths = (D_MODEL, A_WIDTH, A_WIDTH)
    consts = [wts["w_o"], wts["ffn_norm"], wts["router_hi"], wts["router_lo"], wts["router_b"]]
    out_shape = [jax.ShapeDtypeStruct((total, D_MODEL), F32), jax.ShapeDtypeStruct((total * CHUNKS, LANES), F32),
                 jax.ShapeDtypeStruct((total, LANES), F32), jax.ShapeDtypeStruct((16, total), F32),
                 jax.ShapeDtypeStruct((8, LANES), F32)]
    out_specs = [row(D_MODEL), pl.BlockSpec((tm * CHUNKS, LANES), lambda i: (i, 0)), row(LANES),
                 pl.BlockSpec((16, tm), lambda i: (0, i)), pl.BlockSpec((8, LANES), lambda i: (0, 0))]
    return pl.pallas_call(
        functools.partial(_post_kernel, prompt_tiles=n_p),
        grid=(n_p + n_s,),
        in_specs=[row_p(w) for w in widths] + [row_s(w) for w in widths] + [full(c) for c in consts],
        out_specs=out_specs,
        out_shape=out_shape,
        scratch_shapes=[pltpu.VMEM((8, LANES), F32)],
        compiler_params=pltpu.CompilerParams(dimension_semantics=("arbitrary",), vmem_limit_bytes=_vmem(56)),
        name="post",
    )(*prompt, *sample, *consts)


def _dispatch_kernel(pos_ref, hn_ref, xs_ref, sem):
    tm = DISPATCH_TM

    def copy(t, slot):
        return pltpu.make_async_copy(hn_ref.at[t], xs_ref.at[slot], sem)

    def issue(t, c):
        for k in range(TOP_K):
            copy(t, pos_ref[0, 0, k * tm + t]).start(priority=k % 2)
        return c

    def drain(t, c):
        for k in range(TOP_K):
            copy(0, 0).wait()
        return c

    lax.fori_loop(0, tm, issue, 0, unroll=DMA_UNROLL)
    lax.fori_loop(0, tm, drain, 0, unroll=DMA_UNROLL)


def _dispatch(pos_blocks, hn, rows):
    tm = DISPATCH_TM
    hn3 = hn.reshape(-1, CHUNKS, LANES)
    return pl.pallas_call(
        _dispatch_kernel,
        grid=(hn3.shape[0] // tm,),
        in_specs=[pl.BlockSpec((1, 1, TOP_K * tm), lambda i: (i, 0, 0), memory_space=pltpu.SMEM),
                  pl.BlockSpec((tm, CHUNKS, LANES), lambda i: (i, 0, 0))],
        out_specs=pl.BlockSpec(memory_space=pl.ANY),
        out_shape=jax.ShapeDtypeStruct((rows, CHUNKS, LANES), F32),
        scratch_shapes=[pltpu.SemaphoreType.DMA(())],
        compiler_params=pltpu.CompilerParams(dimension_semantics=("arbitrary",)),
        name="dispatch",
    )(pos_blocks, hn3)


def _combine_kernel(pos_ref, pos_next_ref, ys_ref, x1_ref, route_ref, g_ref, o_ref, ybuf, sem, *, steps):
    tm = COMBINE_TM
    step = pl.program_id(0)
    cur = step % 2

    def copy(slot, buf, k, t):
        dst = ybuf.at[buf, k, pl.ds(pl.multiple_of(t * CHUNKS, CHUNKS), CHUNKS)]
        return pltpu.make_async_copy(ys_ref.at[slot], dst, sem.at[buf])

    def fetch(index_ref, buf):
        def issue(t, c):
            for k in range(TOP_K):
                copy(index_ref[0, 0, k * tm + t], buf, k, t).start(priority=k % 2)
            return c

        lax.fori_loop(0, tm, issue, 0, unroll=DMA_UNROLL)

    @pl.when(step == 0)
    def _():
        fetch(pos_ref, 0)

    @pl.when(step + 1 < steps)
    def _():
        fetch(pos_next_ref, 1 - cur)

    def drain(t, c):
        for k in range(TOP_K):
            copy(0, cur, k, 0).wait()
        return c

    lax.fori_loop(0, tm, drain, 0, unroll=DMA_UNROLL)
    route = route_ref[...]
    y = x1_ref[...]
    for k in range(TOP_K):
        y = y + route[:, 2 * TOP_K + k:2 * TOP_K + k + 1] * _load_token_tiles(ybuf.at[cur, k], tm)
    o_ref[...] = _rms(y, g_ref[...])


def _combine(pos_blocks, ys, x1, route, final_norm, row_off, rows):
    tm = COMBINE_TM
    off = row_off // tm
    last = off + rows // tm - 1
    pos_spec = lambda ahead: pl.BlockSpec((1, 1, TOP_K * tm), lambda i: (jnp.minimum(i + off + ahead, last), 0, 0),
                                          memory_space=pltpu.SMEM)
    return pl.pallas_call(
        functools.partial(_combine_kernel, steps=rows // tm),
        grid=(rows // tm,),
        in_specs=[pos_spec(0), pos_spec(1),
                  pl.BlockSpec(memory_space=pl.ANY),
                  pl.BlockSpec((tm, D_MODEL), lambda i: (i + off, 0)),
                  pl.BlockSpec((tm, LANES), lambda i: (i + off, 0)),
                  pl.BlockSpec((1, D_MODEL), lambda i: (0, 0))],
        out_specs=pl.BlockSpec((tm, D_MODEL), lambda i: (i, 0)),
        out_shape=jax.ShapeDtypeStruct((rows, D_MODEL), F32),
        scratch_shapes=[pltpu.VMEM((2, TOP_K, tm * CHUNKS, LANES), F32), pltpu.SemaphoreType.DMA((2,))],
        compiler_params=pltpu.CompilerParams(dimension_semantics=("arbitrary",), vmem_limit_bytes=_vmem(32)),
        name="combine",
    )(pos_blocks, pos_blocks, ys, x1, route, final_norm)


def _ffn_kernel(tile_ref, expert_ref, npairs_ref, x_ref, wgu_ref, bgu_ref, wd_ref, bd_ref, y_ref, wgu_bf, wd_bf):
    step = pl.program_id(0)
    n = FFN_TM // FFN_SUB

    @pl.when((step == 0) | (expert_ref[step] != expert_ref[jnp.maximum(step - 1, 0)]))
    def _():
        for r in range(0, D_MODEL, FFN_CAST_ROWS):
            rows = slice(r, r + FFN_CAST_ROWS)
            wgu_bf[rows, :] = wgu_ref[0, rows, :].astype(BF16)
            wd_bf[rows, :] = wd_ref[0, rows, :].astype(BF16)

    @pl.when(step >= npairs_ref[0])
    def _():
        y_ref[...] = jnp.zeros_like(y_ref)

    @pl.when(step < npairs_ref[0])
    def _():
        for h in range(FFN_SUB):
            xb = _load_token_tiles(x_ref, n, h * n).astype(BF16)
            gate = jnp.dot(xb, wgu_bf[:, :D_FF], preferred_element_type=F32) + bgu_ref[0, :, :D_FF]
            up = jnp.dot(xb, wgu_bf[:, D_FF:], preferred_element_type=F32) + bgu_ref[0, :, D_FF:]
            gate = jnp.minimum(gate, SWIGLU_LIMIT)
            up = jnp.clip(up, -SWIGLU_LIMIT, SWIGLU_LIMIT)
            act = (up + 1.0) * gate * (1.0 / (1.0 + jnp.exp(-SWIGLU_ALPHA * gate)))
            res = jnp.dot(act.astype(BF16), wd_bf[...], preferred_element_type=F32) + bd_ref[0]
            _store_token_tiles(y_ref, res, h * n)


def _ffn(pairs, xs, wts):
    tm = FFN_TM
    pair_tile, pair_expert, n_pairs = pairs
    slots = pair_tile.shape[0]
    x_map = lambda i, pt, pe, n: (pt[i], 0)
    w_map = lambda i, pt, pe, n: (pe[i], 0, 0)
    grid_spec = pltpu.PrefetchScalarGridSpec(
        num_scalar_prefetch=3,
        grid=(slots,),
        in_specs=[pl.BlockSpec((tm * CHUNKS, LANES), x_map),
                  pl.BlockSpec((1, D_MODEL, 2 * D_FF), w_map),
                  pl.BlockSpec((1, 1, 2 * D_FF), w_map),
                  pl.BlockSpec((1, D_FF, D_MODEL), w_map),
                  pl.BlockSpec((1, 1, D_MODEL), w_map)],
        out_specs=pl.BlockSpec((tm * CHUNKS, LANES), lambda i, pt, pe, n: (i, 0)),
        scratch_shapes=[pltpu.VMEM((D_MODEL, 2 * D_FF), BF16), pltpu.VMEM((D_FF, D_MODEL), BF16)],
    )
    ys = pl.pallas_call(
        _ffn_kernel,
        grid_spec=grid_spec,
        out_shape=jax.ShapeDtypeStruct((slots * tm * CHUNKS, LANES), F32),
        compiler_params=pltpu.CompilerParams(dimension_semantics=("arbitrary",), vmem_limit_bytes=_vmem(58)),
        name="ffn",
    )(pair_tile, pair_expert, n_pairs, xs.reshape(-1, LANES), wts["w_gate_up"], wts["b_gate_up"],
      wts["w_down"], wts["b_down"])
    return ys.reshape(slots * tm, CHUNKS, LANES)


def _rope_tables(seq):
    pos = jnp.arange(seq, dtype=F32)[:, None]

    def cs(dim):
        inv = 1.0 / (ROPE_THETA ** (jnp.arange(0, dim, 2, dtype=F32) / dim))
        ang = pos * inv[None, :]
        return jnp.cos(ang), jnp.sin(ang)

    ca, sa = cs(HEAD_DIM)
    cb, sb = cs(ROPE_DIM)
    one = jnp.ones((seq, NOPE_DIM), F32)
    pad1 = jnp.ones((seq, LANES - QK_DIM), F32)
    cosa = jnp.concatenate([ca, ca, ca, ca], axis=1)
    sina = jnp.concatenate([-sa, sa, -sa, sa], axis=1)
    cosb = jnp.concatenate([one, cb, cb, pad1], axis=1)
    sinb = jnp.concatenate([0.0 * one, -sb, sb, 0.0 * pad1], axis=1)
    return cosa, sina, cosb, sinb


def _prep_weights(attn_norm, w_in, q_norm, w_uq, kv_norm, w_ukv, w_o, ffn_norm, router_w, router_b,
                  w_gate_up, b_gate_up, w_down, b_down):
    w = w_in[0]
    zeros = lambda n: jnp.zeros((D_MODEL, n), F32)
    krope = jnp.concatenate([zeros(NOPE_DIM), w[:, 2176:], zeros(LANES - QK_DIM)], axis=1)
    w_in_p = jnp.concatenate([w[:, :2176], krope], axis=1).astype(BF16)
    uq = w_uq[0].reshape(Q_RANK, N_HEADS, QK_DIM)
    uq = jnp.pad(uq, ((0, 0), (0, 0), (0, LANES - QK_DIM))).reshape(Q_RANK, N_HEADS * LANES).astype(BF16)
    ukv = w_ukv[0].reshape(KV_RANK, N_HEADS, NOPE_DIM + V_DIM)
    pad = lambda a: jnp.pad(a, ((0, 0), (0, 0), (0, LANES - a.shape[2]))).reshape(KV_RANK, N_HEADS * LANES)
    ukv = jnp.concatenate([pad(ukv[:, :, :NOPE_DIM]), pad(ukv[:, :, NOPE_DIM:])], axis=1).astype(BF16)
    rw = jnp.pad(router_w[0], ((0, 0), (0, LANES - N_EXPERTS)))
    rw_hi = rw.astype(BF16)
    rw_lo = (rw - rw_hi.astype(F32)).astype(BF16)
    rb = jnp.concatenate([router_b[0], jnp.full((LANES - N_EXPERTS,), NEG_INF, F32)])[None, :]
    return {
        "attn_norm": attn_norm[0][None, :], "w_in": w_in_p, "q_norm": q_norm[0][None, :], "w_uq": uq,
        "kv_norm": kv_norm[0][None, :], "w_ukv": ukv, "w_o": w_o[0].astype(BF16), "ffn_norm": ffn_norm[0][None, :],
        "router_hi": rw_hi, "router_lo": rw_lo, "router_b": rb,
        "w_gate_up": w_gate_up[0], "b_gate_up": b_gate_up[0][:, None, :],
        "w_down": w_down[0], "b_down": b_down[0][:, None, :],
    }


def _routing(route_t, counts, total):
    i32 = jnp.int32
    idx = route_t[0:TOP_K].astype(i32)
    rank = route_t[TOP_K:2 * TOP_K].astype(i32)
    cnt = counts[0, :N_EXPERTS].astype(i32)
    seg_end = jnp.cumsum(cnt)
    seg_start = seg_end - cnt
    experts = jnp.arange(N_EXPERTS, dtype=i32)
    per_expert = lambda table: jnp.sum(jnp.where(idx[None] == experts[:, None, None], table[:, None, None], 0), axis=0)
    pos = rank + per_expert(seg_start)

    first_tile = seg_start // FFN_TM
    n_per = jnp.where(cnt > 0, (seg_end - 1) // FFN_TM - first_tile + 1, 0)
    pair_end = jnp.cumsum(n_per)
    n_pairs = pair_end[-1]
    pos_out = pos + per_expert((pair_end - n_per - first_tile) * FFN_TM)
    slots = total * TOP_K // FFN_TM + N_EXPERTS
    j = jnp.minimum(jnp.arange(slots, dtype=i32), n_pairs - 1)
    onehot = (jnp.sum((j[:, None] >= pair_end[None, :]).astype(i32), axis=1)[:, None] == experts[None, :]).astype(i32)
    pick = lambda table: jnp.sum(onehot * table[None, :], axis=1)
    pair_expert = pick(experts)
    pair_tile = pick(first_tile) + j - pick(pair_end - n_per)

    def blocks(p, tm):
        return p.reshape(TOP_K, total // tm, tm).transpose(1, 0, 2).reshape(total // tm, 1, TOP_K * tm)

    return blocks(pos, DISPATCH_TM), blocks(pos_out, COMBINE_TM), (pair_tile, pair_expert, n_pairs[None].astype(i32))


def _mixers(x, wts):
    batch, seq, _ = x.shape
    x2d = x.reshape(batch * seq, D_MODEL)
    qa, ka, va, qb, kb, vbt = _proj(x2d, seq, wts, _rope_tables(seq))
    oa = _dilated_all(qa, ka, va, batch, seq)
    ob = _attn_b(qb, kb, vbt, batch, seq)
    return x2d, oa, ob


def kernel(x_prompt, x_sample, attn_norm, w_in, q_norm, w_uq, kv_norm, w_ukv, w_o, ffn_norm, router_w, router_b,
           w_gate_up, b_gate_up, w_down, b_down, final_norm):
    wts = _prep_weights(attn_norm, w_in, q_norm, w_uq, kv_norm, w_ukv, w_o, ffn_norm, router_w, router_b,
                        w_gate_up, b_gate_up, w_down, b_down)
    sets = [x_prompt, x_sample]
    rows = [x.shape[0] * x.shape[1] for x in sets]
    total = sum(rows)
    x1, hn, route, route_t, counts = _post(_mixers(x_prompt, wts), _mixers(x_sample, wts), wts)
    pos_dispatch, pos_combine, pairs = _routing(route_t, counts, total)
    xs = _dispatch(pos_dispatch, hn, total * TOP_K)
    ys = _ffn(pairs, xs, wts)
    outs = []
    off = 0
    fnorm = final_norm[None, :]
    for x, n in zip(sets, rows):
        outs.append(_combine(pos_combine, ys, x1, route, fnorm, off, n).reshape(x.shape))
        off += n
    return tuple(outs)
```

```python
import functools

import jax
import jax.numpy as jnp
from jax import lax
from jax.experimental import pallas as pl
from jax.experimental.pallas import tpu as pltpu

D_MODEL = 1024
N_HEADS = 8
HEAD_DIM = 64
A_WIDTH = N_HEADS * HEAD_DIM
NOPE_DIM = 64
ROPE_DIM = 32
QK_DIM = NOPE_DIM + ROPE_DIM
V_DIM = 64
Q_RANK = 384
KV_RANK = 256
DILATIONS = (1, 4, 16)
BAND = 64
N_EXPERTS = 32
TOP_K = 4
D_FF = 1024
SWIGLU_LIMIT = 7.0
SWIGLU_ALPHA = 1.702
ROPE_THETA = 10000.0
NORM_EPS = 1e-5
NEG_INF = -1e30

LANES = 128
IN_SPLITS = (0, 512, 1024, 1536, 1920, 2176, 2304)

PROJ_TM = 512
ATTN_TQ = 512
ATTN_TK = 256
ATTN_SCORE_BYTES = 24 * 1024 * 1024
DIL_QB = 128
DIL_UNROLL = 16
POST_TM = 512
DISPATCH_TM = 512
DMA_UNROLL = 4
FFN_TM = 512
FFN_SUB = 2
FFN_CAST_ROWS = 256
COMBINE_TM = 512

F32 = jnp.float32
BF16 = jnp.bfloat16


def _vmem(mib):
    return mib * 1024 * 1024


def _rms(x, g):
    return x * lax.rsqrt(jnp.mean(x * x, axis=-1, keepdims=True) + NORM_EPS) * g


CHUNKS = D_MODEL // LANES


def _load_token_tiles(ref, n, first=0):
    rows = lambda c: pl.ds(first * CHUNKS + c, n, stride=CHUNKS)
    return jnp.concatenate([ref[rows(c), :] for c in range(CHUNKS)], axis=1)


def _store_token_tiles(ref, x, first=0):
    for c in range(CHUNKS):
        ref[pl.ds(first * CHUNKS + c, x.shape[0], stride=CHUNKS), :] = x[:, c * LANES:(c + 1) * LANES]


def _proj_kernel(x_ref, g_ref, win_ref, qn_ref, wuq_ref, kvn_ref, wukv_ref,
                 cosa_ref, sina_ref, cosb_ref, sinb_ref,
                 qa_ref, ka_ref, va_ref, qb_ref, kb_ref, vbt_ref):
    tm = x_ref.shape[0]
    hb = _rms(x_ref[...], g_ref[...]).astype(BF16)

    def mm(g):
        return jnp.dot(hb, win_ref[:, IN_SPLITS[g]:IN_SPLITS[g + 1]], preferred_element_type=F32)

    lane_a = lax.broadcasted_iota(jnp.int32, (tm, A_WIDTH), 1)
    first_half = (lane_a % HEAD_DIM) < (HEAD_DIM // 2)
    cosa = jnp.tile(cosa_ref[...], (1, A_WIDTH // LANES))
    sina = jnp.tile(sina_ref[...], (1, A_WIDTH // LANES))

    def rope_a(t):
        sw = jnp.where(first_half, pltpu.roll(t, A_WIDTH - HEAD_DIM // 2, 1), pltpu.roll(t, HEAD_DIM // 2, 1))
        return t * cosa + sw * sina

    qa_ref[...] = rope_a(mm(0)) * (HEAD_DIM ** -0.5)
    ka_ref[...] = rope_a(mm(1))
    va_ref[...] = mm(2)

    def rope_b(t, cosb, sinb):
        w = t.shape[1]
        lane = lax.broadcasted_iota(jnp.int32, t.shape, 1) % LANES
        half = ROPE_DIM // 2
        sw = jnp.where(lane < NOPE_DIM + half, pltpu.roll(t, w - half, 1), pltpu.roll(t, half, 1))
        return t * cosb + sw * sinb

    cq = _rms(mm(3), qn_ref[...]).astype(BF16)
    qb = jnp.dot(cq, wuq_ref[...], preferred_element_type=F32)
    cosb8 = jnp.tile(cosb_ref[...], (1, N_HEADS))
    sinb8 = jnp.tile(sinb_ref[...], (1, N_HEADS))
    qb_ref[...] = (rope_b(qb, cosb8, sinb8) * (QK_DIM ** -0.5)).astype(BF16)

    ckv = _rms(mm(4), kvn_ref[...]).astype(BF16)
    kv = jnp.dot(ckv, wukv_ref[...], preferred_element_type=F32)
    kpe = rope_b(mm(5), cosb_ref[...], sinb_ref[...])
    kb_ref[...] = (kv[:, :N_HEADS * LANES] + jnp.tile(kpe, (1, N_HEADS))).astype(BF16)
    lane_v = lax.broadcasted_iota(jnp.int32, (tm, N_HEADS * LANES), 1) % LANES
    vbt_ref[0] = jnp.where(lane_v == V_DIM, 1.0, kv[:, N_HEADS * LANES:]).T.astype(BF16)


def _proj(x2d, seq, wts, tables):
    t = x2d.shape[0]
    tm = PROJ_TM
    n_seq_tiles = seq // tm
    cosa, sina, cosb, sinb = tables
    full = lambda a: pl.BlockSpec(a.shape, lambda i: (0,) * a.ndim)
    tab = pl.BlockSpec((tm, LANES), lambda i: (i % n_seq_tiles, 0))
    row = lambda w: pl.BlockSpec((tm, w), lambda i: (i, 0))
    outs = [A_WIDTH, A_WIDTH, A_WIDTH, N_HEADS * LANES, N_HEADS * LANES]
    vt_spec = pl.BlockSpec((1, N_HEADS * LANES, tm), lambda i: (i // n_seq_tiles, 0, i % n_seq_tiles))
    vt_shape = jax.ShapeDtypeStruct((t // seq, N_HEADS * LANES, seq), BF16)
    return pl.pallas_call(
        _proj_kernel,
        grid=(t // tm,),
        in_specs=[row(D_MODEL), full(wts["attn_norm"]), full(wts["w_in"]), full(wts["q_norm"]), full(wts["w_uq"]),
                  full(wts["kv_norm"]), full(wts["w_ukv"]), tab, tab, tab, tab],
        out_specs=[row(w) for w in outs] + [vt_spec],
        out_shape=[jax.ShapeDtypeStruct((t, w), F32 if j < 3 else BF16) for j, w in enumerate(outs)] + [vt_shape],
        compiler_params=pltpu.CompilerParams(dimension_semantics=("arbitrary",), vmem_limit_bytes=_vmem(56)),
        name="proj",
    )(x2d, wts["attn_norm"], wts["w_in"], wts["q_norm"], wts["w_uq"], wts["kv_norm"], wts["w_ukv"],
      cosa, sina, cosb, sinb)


def _attn_b_kernel(q_ref, k_ref, vt_ref, o_ref, st_ref, p_ref, *, seq):
    tq = q_ref.shape[1]
    n_heads = st_ref.shape[0]
    chunks = [slice(i * ATTN_TK, (i + 1) * ATTN_TK) for i in range(seq // ATTN_TK)]
    outs = []
    for j in range(n_heads):
        lanes = slice(LANES * j, LANES * (j + 1))
        st_ref[j] = lax.dot_general(k_ref[0, :, lanes], q_ref[0, :, lanes], (((1,), (1,)), ((), ())),
                                    preferred_element_type=F32)
    for j in range(n_heads):
        lanes = slice(LANES * j, LANES * (j + 1))
        m = jnp.full((1, tq), NEG_INF, F32)
        for rows in chunks:
            m = jnp.maximum(m, jnp.max(st_ref[j, rows, :], axis=0, keepdims=True))
        for rows in chunks:
            p_ref[j, rows, :] = jnp.exp(st_ref[j, rows, :] - m).astype(BF16)
        vt = vt_ref[0, LANES * j:LANES * j + V_DIM + 16, :]
        acc = jnp.dot(vt, p_ref[j], preferred_element_type=F32)
        outs.append((acc / acc[V_DIM:V_DIM + 1, :])[:V_DIM])
    o_ref[0] = jnp.concatenate(outs, axis=0).T.astype(BF16)


def _attn_b(qb, kb, vbt, batch, seq):
    q3 = qb.reshape(batch, seq, N_HEADS * LANES)
    k3 = kb.reshape(batch, seq, N_HEADS * LANES)
    tq = ATTN_TQ
    heads = max(2, min(N_HEADS, ATTN_SCORE_BYTES // (6 * seq * tq)))
    width = heads * LANES
    out = pl.pallas_call(
        functools.partial(_attn_b_kernel, seq=seq),
        grid=(batch, N_HEADS // heads, seq // tq),
        in_specs=[pl.BlockSpec((1, tq, width), lambda b, h, i: (b, i, h)),
                  pl.BlockSpec((1, seq, width), lambda b, h, i: (b, 0, h)),
                  pl.BlockSpec((1, width, seq), lambda b, h, i: (b, h, 0))],
        out_specs=pl.BlockSpec((1, tq, heads * V_DIM), lambda b, h, i: (b, i, h)),
        out_shape=jax.ShapeDtypeStruct((batch, seq, N_HEADS * V_DIM), BF16),
        scratch_shapes=[pltpu.VMEM((heads, seq, tq), F32), pltpu.VMEM((heads, seq, tq), BF16)],
        compiler_params=pltpu.CompilerParams(dimension_semantics=("arbitrary",) * 3, vmem_limit_bytes=_vmem(56)),
        name="attn_b",
    )(q3, k3, vbt)
    return out.reshape(batch * seq, N_HEADS * V_DIM)


def _dil_all_kernel(bias_ref, q_ref, k_ref, v_ref, o_ref, acc_ref, m_ref, l_ref, *, seq):
    qb_rows = DIL_QB
    lane = lax.broadcasted_iota(jnp.int32, (qb_rows, LANES), 1)
    low = lane < HEAD_DIM
    for dil in DILATIONS:
        length = seq // dil
        win = min(2 * qb_rows, length)
        nblk = length // qb_rows

        def body(n, carry, dil=dil, length=length, win=win, nblk=nblk):
            res = n // nblk
            r0 = (n % nblk) * qb_rows
            start = jnp.clip(r0 - BAND, 0, length - win)
            if dil == 1:
                qrows = pl.ds(pl.multiple_of(r0, qb_rows), qb_rows)
                krows = pl.ds(pl.multiple_of(start, BAND), win)
            else:
                qrows = pl.ds(res + dil * r0, qb_rows, stride=dil)
                krows = pl.ds(res + dil * start, win, stride=dil)
            case = jnp.where(r0 == 0, 0, jnp.where(r0 == length - qb_rows, 2, 1))
            bias = bias_ref[case, :, :win]
            qp = q_ref[qrows, :]
            zero = jnp.zeros_like(qp)
            qq = jnp.concatenate([jnp.where(low, qp, zero), jnp.where(low, zero, qp)], axis=0).astype(BF16)
            kw = k_ref[krows, :].astype(BF16)
            vw = v_ref[krows, :].astype(BF16)
            s = lax.dot_general(qq, kw, (((1,), (1,)), ((), ())), preferred_element_type=F32) + bias
            m = jnp.max(s, axis=-1, keepdims=True)
            p = jnp.exp(s - m)
            l = jnp.sum(p, axis=-1, keepdims=True)
            pv = jnp.dot(p.astype(BF16), vw, preferred_element_type=F32)
            pick =lambda t: jnp.where(low, t[:qb_rows], t[qb_rows:])
            m_new, l_new, pv_new = pick(m), pick(l), pick(pv)
            if dil == DILATIONS[0]:
                m_ref[qrows, :] = m_new
                l_ref[qrows, :] = l_new
                acc_ref[qrows, :] = pv_new
            else:
                m_old = m_ref[qrows, :]
                top = jnp.maximum(m_old, m_new)
                a = jnp.exp(m_old - top)
                b = jnp.exp(m_new - top)
                m_ref[qrows, :] = top
                l_ref[qrows, :] = a * l_ref[qrows, :] + b * l_new
                acc_ref[qrows, :] = a * acc_ref[qrows, :] + b * pv_new
            return carry

        lax.fori_loop(0, dil * nblk, body, 0, unroll=DIL_UNROLL)
    for c in range(seq // 512):
        rows = slice(c * 512, (c + 1) * 512)
        o_ref[rows, :] = (acc_ref[rows, :] / l_ref[rows, :]).astype(BF16)


def _band_bias():
    q = jnp.arange(2 * DIL_QB, dtype=jnp.int32)[None, :, None] % DIL_QB
    k = jnp.arange(2 * DIL_QB, dtype=jnp.int32)[None, None, :]
    delta = jnp.array([0, -BAND, -DIL_QB], jnp.int32)[:, None, None]
    return jnp.where(jnp.abs(k + delta - q) <= BAND, 0.0, NEG_INF).astype(F32)


def _dilated_all(qa, ka, va, batch, seq):
    view = lambda a: a.reshape(batch, seq, A_WIDTH)
    spec = pl.BlockSpec((None, seq, LANES), lambda b, h: (b, 0, h))
    bias = _band_bias()
    out = pl.pallas_call(
        functools.partial(_dil_all_kernel, seq=seq),
        grid=(batch, A_WIDTH // LANES),
        in_specs=[pl.BlockSpec(bias.shape, lambda b, h: (0, 0, 0)), spec, spec, spec],
        out_specs=spec,
        out_shape=jax.ShapeDtypeStruct((batch, seq, A_WIDTH), BF16),
        scratch_shapes=[pltpu.VMEM((seq, LANES), F32)] * 3,
        compiler_params=pltpu.CompilerParams(dimension_semantics=("arbitrary",) * 2, vmem_limit_bytes=_vmem(48)),
        name="dilated",
    )(bias, view(qa), view(ka), view(va))
    return out.reshape(batch * seq, A_WIDTH)


def _post_kernel(xp_ref, oap_ref, obp_ref, xs_ref, oas_ref, obs_ref, wo_ref, g_ref, rwh_ref, rwl_ref, rb_ref,
                 x1_ref, hn_ref, route_ref, routet_ref, cout_ref, carry_ref, *, prompt_tiles):
    tm = xp_ref.shape[0]
    is_prompt = pl.program_id(0) < prompt_tiles

    @pl.when(pl.program_id(0) == 0)
    def _():
        carry_ref[...] = jnp.zeros_like(carry_ref)

    oa = jnp.where(is_prompt, oap_ref[...], oas_ref[...])
    ob = jnp.where(is_prompt, obp_ref[...], obs_ref[...])
    attn = jnp.dot(oa, wo_ref[:A_WIDTH, :], preferred_element_type=F32)
    attn += jnp.dot(ob, wo_ref[A_WIDTH:, :], preferred_element_type=F32)
    x1 = jnp.where(is_prompt, xp_ref[...], xs_ref[...]) + attn
    x1_ref[...] = x1
    hn = _rms(x1, g_ref[...])
    _store_token_tiles(hn_ref, hn)

    hi = hn.astype(BF16)
    lo = (hn - hi.astype(F32)).astype(BF16)
    logits = jnp.dot(hi, rwh_ref[...], preferred_element_type=F32)
    logits += jnp.dot(lo, rwh_ref[...], preferred_element_type=F32)
    logits += jnp.dot(hi, rwl_ref[...], preferred_element_type=F32)
    logits += rb_ref[...]

    lane = lax.broadcasted_iota(jnp.int32, (tm, LANES), 1)
    work = logits
    vals, sels = [], []
    for _ in range(TOP_K):
        mx = jnp.max(work, axis=-1, keepdims=True)
        first = jnp.min(jnp.where(work == mx, lane, LANES), axis=-1, keepdims=True)
        sel = lane == first
        work = jnp.where(sel, -jnp.inf, work)
        vals.append(mx)
        sels.append(sel)
    exps = [jnp.exp(v - vals[0]) for v in vals]
    den = exps[0] + exps[1] + exps[2] + exps[3]

    sel_all = (sels[0] | sels[1] | sels[2] | sels[3]).astype(F32)
    tri = (lax.broadcasted_iota(jnp.int32, (tm, tm), 1) < lax.broadcasted_iota(jnp.int32, (tm, tm), 0)).astype(BF16)
    before = jnp.dot(tri, sel_all.astype(BF16), preferred_element_type=F32) + carry_ref[0:1, :]
    carry_ref[0:1, :] = carry_ref[0:1, :] + jnp.sum(sel_all, axis=0, keepdims=True)
    cout_ref[...] = carry_ref[...]

    lane_f = lane.astype(F32)
    route = jnp.zeros((tm, LANES), F32)
    for k in range(TOP_K):
        idx = jnp.sum(jnp.where(sels[k], lane_f, 0.0), axis=-1, keepdims=True)
        rank = jnp.sum(jnp.where(sels[k], before, 0.0), axis=-1, keepdims=True)
        route = jnp.where(lane == k, idx, route)
        route = jnp.where(lane == TOP_K + k, rank, route)
        route = jnp.where(lane == 2 * TOP_K + k, exps[k] / den, route)
    route_ref[...] = route
    routet_ref[...] = route.T[:16, :]


def _post(prompt, sample, wts):
    tm = POST_TM
    n_p = prompt[0].shape[0] // tm
    n_s = sample[0].shape[0] // tm
    total = (n_p + n_s) * tm
    full = lambda a: pl.BlockSpec(a.shape, lambda i: (0,) * a.ndim)
    row_p = lambda w: pl.BlockSpec((tm, w), lambda i: (jnp.minimum(i, n_p - 1), 0))
    row_s = lambda w: pl.BlockSpec((tm, w), lambda i: (jnp.maximum(i - n_p, 0), 0))
    row = lambda w: pl.BlockSpec((tm, w), lambda i: (i, 0))
    widths = (D_MODEL, A_WIDTH, A_WIDTH)
    consts = [wts["w_o"], wts["ffn_norm"], wts["router_hi"], wts["router_lo"], wts["router_b"]]
    out_shape = [jax.ShapeDtypeStruct((total, D_MODEL), F32), jax.ShapeDtypeStruct((total * CHUNKS, LANES), F32),
                 jax.ShapeDtypeStruct((total, LANES), F32), jax.ShapeDtypeStruct((16, total), F32),
                 jax.ShapeDtypeStruct((8, LANES), F32)]
    out_specs = [row(D_MODEL), pl.BlockSpec((tm * CHUNKS, LANES), lambda i: (i, 0)), row(LANES),
                 pl.BlockSpec((16, tm), lambda i: (0, i)), pl.BlockSpec((8, LANES), lambda i: (0, 0))]
    return pl.pallas_call(
        functools.partial(_post_kernel, prompt_tiles=n_p),
        grid=(n_p + n_s,),
        in_specs=[row_p(w) for w in widths] + [row_s(w) for w in widths] + [full(c) for c in consts],
        out_specs=out_specs,
        out_shape=out_shape,
        scratch_shapes=[pltpu.VMEM((8, LANES), F32)],
        compiler_params=pltpu.CompilerParams(dimension_semantics=("arbitrary",), vmem_limit_bytes=_vmem(56)),
        name="post",
    )(*prompt, *sample, *consts)


def _dispatch_kernel(pos_ref, hn_ref, xs_ref, sem):
    tm = DISPATCH_TM

    def copy(t, slot):
        return pltpu.make_async_copy(hn_ref.at[t], xs_ref.at[slot], sem)

    def issue(t, c):
        for k in range(TOP_K):
            copy(t, pos_ref[0, 0, k * tm + t]).start(priority=k % 2)
        return c

    def drain(t, c):
        for k in range(TOP_K):
            copy(0, 0).wait()
        return c

    lax.fori_loop(0, tm, issue, 0, unroll=DMA_UNROLL)
    lax.fori_loop(0, tm, drain, 0, unroll=DMA_UNROLL)


def _dispatch(pos_blocks, hn, rows):
    tm = DISPATCH_TM
    hn3 = hn.reshape(-1, CHUNKS, LANES)
    return pl.pallas_call(
        _dispatch_kernel,
        grid=(hn3.shape[0] // tm,),
        in_specs=[pl.BlockSpec((1, 1, TOP_K * tm), lambda i: (i, 0, 0), memory_space=pltpu.SMEM),
                  pl.BlockSpec((tm, CHUNKS, LANES), lambda i: (i, 0, 0))],
        out_specs=pl.BlockSpec(memory_space=pl.ANY),
        out_shape=jax.ShapeDtypeStruct((rows, CHUNKS, LANES), F32),
        scratch_shapes=[pltpu.SemaphoreType.DMA(())],
        compiler_params=pltpu.CompilerParams(dimension_semantics=("arbitrary",)),
        name="dispatch",
    )(pos_blocks, hn3)


def _combine_kernel(pos_ref, pos_next_ref, ys_ref, x1_ref, route_ref, g_ref, o_ref, ybuf, sem, *, steps):
    tm = COMBINE_TM
    step = pl.program_id(0)
    cur = step % 2

    def copy(slot, buf, k, t):
        dst = ybuf.at[buf, k, pl.ds(pl.multiple_of(t * CHUNKS, CHUNKS), CHUNKS)]
        return pltpu.make_async_copy(ys_ref.at[slot], dst, sem.at[buf])

    def fetch(index_ref, buf):
        def issue(t, c):
            for k in range(TOP_K):
                copy(index_ref[0, 0, k * tm + t], buf, k, t).start(priority=k % 2)
            return c

        lax.fori_loop(0, tm, issue, 0, unroll=DMA_UNROLL)

    @pl.when(step == 0)
    def _():
        fetch(pos_ref, 0)

    @pl.when(step + 1 < steps)
    def _():
        fetch(pos_next_ref, 1 - cur)

    def drain(t, c):
        for k in range(TOP_K):
            copy(0, cur, k, 0).wait()
        return c

    lax.fori_loop(0, tm, drain, 0, unroll=DMA_UNROLL)
    route = route_ref[...]
    y = x1_ref[...]
    for k in range(TOP_K):
        y = y + route[:, 2 * TOP_K + k:2 * TOP_K + k + 1] * _load_token_tiles(ybuf.at[cur, k], tm)
    o_ref[...] = _rms(y, g_ref[...])


def _combine(pos_blocks, ys, x1, route, final_norm, row_off, rows):
    tm = COMBINE_TM
    off = row_off // tm
    last = off + rows // tm - 1
    pos_spec = lambda ahead: pl.BlockSpec((1, 1, TOP_K * tm), lambda i: (jnp.minimum(i + off + ahead, last), 0, 0),
                                          memory_space=pltpu.SMEM)
    return pl.pallas_call(
        functools.partial(_combine_kernel, steps=rows // tm),
        grid=(rows // tm,),
        in_specs=[pos_spec(0), pos_spec(1),
                  pl.BlockSpec(memory_space=pl.ANY),
                  pl.BlockSpec((tm, D_MODEL), lambda i: (i + off, 0)),
                  pl.BlockSpec((tm, LANES), lambda i: (i + off, 0)),
                  pl.BlockSpec((1, D_MODEL), lambda i: (0, 0))],
        out_specs=pl.BlockSpec((tm, D_MODEL), lambda i: (i, 0)),
        out_shape=jax.ShapeDtypeStruct((rows, D_MODEL), F32),
        scratch_shapes=[pltpu.VMEM((2, TOP_K, tm * CHUNKS, LANES), F32), pltpu.SemaphoreType.DMA((2,))],
        compiler_params=pltpu.CompilerParams(dimension_semantics=("arbitrary",), vmem_limit_bytes=_vmem(48)),
        name="combine",
    )(pos_blocks, pos_blocks, ys, x1, route, final_norm)


def _ffn_kernel(tile_ref, expert_ref, npairs_ref, x_ref, wgu_ref, bgu_ref, wd_ref, bd_ref, y_ref, wgu_bf, wd_bf):
    step = pl.program_id(0)
    n = FFN_TM // FFN_SUB

    @pl.when((step == 0) | (expert_ref[step] != expert_ref[jnp.maximum(step - 1, 0)]))
    def _():
        for r in range(0, D_MODEL, FFN_CAST_ROWS):
            rows = slice(r, r + FFN_CAST_ROWS)
            wgu_bf[rows, :] = wgu_ref[0, rows, :].astype(BF16)
            wd_bf[rows, :] = wd_ref[0, rows, :].astype(BF16)

    @pl.when(step >= npairs_ref[0])
    def _():
        y_ref[...] = jnp.zeros_like(y_ref)

    @pl.when(step < npairs_ref[0])
    def _():
        for h in range(FFN_SUB):
            xb = _load_token_tiles(x_ref, n, h * n).astype(BF16)
            gate = jnp.dot(xb, wgu_bf[:, :D_FF], preferred_element_type=F32) + bgu_ref[0, :, :D_FF]
            up = jnp.dot(xb, wgu_bf[:, D_FF:], preferred_element_type=F32) + bgu_ref[0, :, D_FF:]
            gate = jnp.minimum(gate, SWIGLU_LIMIT)
            up = jnp.clip(up, -SWIGLU_LIMIT, SWIGLU_LIMIT)
            act = (up + 1.0) * gate * (1.0 / (1.0 + jnp.exp(-SWIGLU_ALPHA * gate)))
            res = jnp.dot(act.astype(BF16), wd_bf[...], preferred_element_type=F32) + bd_ref[0]
            _store_token_tiles(y_ref, res, h * n)


def _ffn(pairs, xs, wts):
    tm = FFN_TM
    pair_tile, pair_expert, n_pairs = pairs
    slots = pair_tile.shape[0]
    x_map = lambda i, pt, pe, n: (pt[i], 0)
    w_map = lambda i, pt, pe, n: (pe[i], 0, 0)
    grid_spec = pltpu.PrefetchScalarGridSpec(
        num_scalar_prefetch=3,
        grid=(slots,),
        in_specs=[pl.BlockSpec((tm * CHUNKS, LANES), x_map),
                  pl.BlockSpec((1, D_MODEL, 2 * D_FF), w_map),
                  pl.BlockSpec((1, 1, 2 * D_FF), w_map),
                  pl.BlockSpec((1, D_FF, D_MODEL), w_map),
                  pl.BlockSpec((1, 1, D_MODEL), w_map)],
        out_specs=pl.BlockSpec((tm * CHUNKS, LANES), lambda i, pt, pe, n: (i, 0)),
        scratch_shapes=[pltpu.VMEM((D_MODEL, 2 * D_FF), BF16), pltpu.VMEM((D_FF, D_MODEL), BF16)],
    )
    ys = pl.pallas_call(
        _ffn_kernel,
        grid_spec=grid_spec,
        out_shape=jax.ShapeDtypeStruct((slots * tm * CHUNKS, LANES), F32),
        compiler_params=pltpu.CompilerParams(dimension_semantics=("arbitrary",), vmem_limit_bytes=_vmem(58)),
        name="ffn",
    )(pair_tile, pair_expert, n_pairs, xs.reshape(-1, LANES), wts["w_gate_up"], wts["b_gate_up"],
      wts["w_down"], wts["b_down"])
    return ys.reshape(slots * tm, CHUNKS, LANES)


def _rope_tables(seq):
    pos = jnp.arange(seq, dtype=F32)[:, None]

    def cs(dim):
        inv = 1.0 / (ROPE_THETA ** (jnp.arange(0, dim, 2, dtype=F32) / dim))
        ang = pos * inv[None, :]
        return jnp.cos(ang), jnp.sin(ang)

    ca, sa = cs(HEAD_DIM)
    cb, sb = cs(ROPE_DIM)
    one = jnp.ones((seq, NOPE_DIM), F32)
    pad1 = jnp.ones((seq, LANES - QK_DIM), F32)
    cosa = jnp.concatenate([ca, ca, ca, ca], axis=1)
    sina = jnp.concatenate([-sa, sa, -sa, sa], axis=1)
    cosb = jnp.concatenate([one, cb, cb, pad1], axis=1)
    sinb = jnp.concatenate([0.0 * one, -sb, sb, 0.0 * pad1], axis=1)
    return cosa, sina, cosb, sinb


def _prep_weights(attn_norm, w_in, q_norm, w_uq, kv_norm, w_ukv, w_o, ffn_norm, router_w, router_b,
                  w_gate_up, b_gate_up, w_down, b_down):
    w = w_in[0]
    zeros = lambda n: jnp.zeros((D_MODEL, n), F32)
    krope = jnp.concatenate([zeros(NOPE_DIM), w[:, 2176:], zeros(LANES - QK_DIM)], axis=1)
    w_in_p = jnp.concatenate([w[:, :2176], krope], axis=1).astype(BF16)
    uq = w_uq[0].reshape(Q_RANK, N_HEADS, QK_DIM)
    uq = jnp.pad(uq, ((0, 0), (0, 0), (0, LANES - QK_DIM))).reshape(Q_RANK, N_HEADS * LANES).astype(BF16)
    ukv = w_ukv[0].reshape(KV_RANK, N_HEADS, NOPE_DIM + V_DIM)
    pad = lambda a: jnp.pad(a, ((0, 0), (0, 0), (0, LANES - a.shape[2]))).reshape(KV_RANK, N_HEADS * LANES)
    ukv = jnp.concatenate([pad(ukv[:, :, :NOPE_DIM]), pad(ukv[:, :, NOPE_DIM:])], axis=1).astype(BF16)
    rw = jnp.pad(router_w[0], ((0, 0), (0, LANES - N_EXPERTS)))
    rw_hi = rw.astype(BF16)
    rw_lo = (rw - rw_hi.astype(F32)).astype(BF16)
    rb = jnp.concatenate([router_b[0], jnp.full((LANES - N_EXPERTS,), NEG_INF, F32)])[None, :]
    return {
        "attn_norm": attn_norm[0][None, :], "w_in": w_in_p, "q_norm": q_norm[0][None, :], "w_uq": uq,
        "kv_norm": kv_norm[0][None, :], "w_ukv": ukv, "w_o": w_o[0].astype(BF16), "ffn_norm": ffn_norm[0][None, :],
        "router_hi": rw_hi, "router_lo": rw_lo, "router_b": rb,
        "w_gate_up": w_gate_up[0], "b_gate_up": b_gate_up[0][:, None, :],
        "w_down": w_down[0], "b_down": b_down[0][:, None, :],
    }


def _routing(route_t, counts, total):
    i32 = jnp.int32
    idx = route_t[0:TOP_K].astype(i32)
    rank = route_t[TOP_K:2 * TOP_K].astype(i32)
    cnt = counts[0, :N_EXPERTS].astype(i32)
    seg_end = jnp.cumsum(cnt)
    seg_start = seg_end - cnt
    experts = jnp.arange(N_EXPERTS, dtype=i32)
    per_expert = lambda table: jnp.sum(jnp.where(idx[None] == experts[:, None, None], table[:, None, None], 0), axis=0)
    pos = rank + per_expert(seg_start)

    first_tile = seg_start // FFN_TM
    n_per = jnp.where(cnt > 0, (seg_end - 1) // FFN_TM - first_tile + 1, 0)
    pair_end = jnp.cumsum(n_per)
    n_pairs = pair_end[-1]
    pos_out = pos + per_expert((pair_end - n_per - first_tile) * FFN_TM)
    slots = total * TOP_K // FFN_TM + N_EXPERTS
    j = jnp.minimum(jnp.arange(slots, dtype=i32), n_pairs - 1)
    onehot = (jnp.sum((j[:, None] >= pair_end[None, :]).astype(i32), axis=1)[:, None] == experts[None, :]).astype(i32)
    pick = lambda table: jnp.sum(onehot * table[None, :], axis=1)
    pair_expert = pick(experts)
    pair_tile = pick(first_tile) + j - pick(pair_end - n_per)

    def blocks(p, tm):
        return p.reshape(TOP_K, total // tm, tm).transpose(1, 0, 2).reshape(total // tm, 1, TOP_K * tm)

    return blocks(pos, DISPATCH_TM), blocks(pos_out, COMBINE_TM), (pair_tile, pair_expert, n_pairs[None].astype(i32))


def _mixers(x, wts):
    batch, seq, _ = x.shape
    x2d = x.reshape(batch * seq, D_MODEL)
    qa, ka, va, qb, kb, vbt = _proj(x2d, seq, wts, _rope_tables(seq))
    oa = _dilated_all(qa, ka, va, batch, seq)
    ob = _attn_b(qb, kb, vbt, batch, seq)
    return x2d, oa, ob


def kernel(x_prompt, x_sample, attn_norm, w_in, q_norm, w_uq, kv_norm, w_ukv, w_o, ffn_norm, router_w, router_b,
           w_gate_up, b_gate_up, w_down, b_down, final_norm):
    wts = _prep_weights(attn_norm, w_in, q_norm, w_uq, kv_norm, w_ukv, w_o, ffn_norm, router_w, router_b,
                        w_gate_up, b_gate_up, w_down, b_down)
    sets = [x_prompt, x_sample]
    rows = [x.shape[0] * x.shape[1] for x in sets]
    total = sum(rows)
    x1, hn, route, route_t, counts = _post(_mixers(x_prompt, wts), _mixers(x_sample, wts), wts)
    pos_dispatch, pos_combine, pairs = _routing(route_t, counts, total)
    xs = _dispatch(pos_dispatch, hn, total * TOP_K)
    ys = _ffn(pairs, xs, wts)
    outs = []
    off = 0
    fnorm = final_norm[None, :]
    for x, n in zip(sets, rows):
        outs.append(_combine(pos_combine, ys, x1, route, fnorm, off, n).reshape(x.shape))
        off += n
    return tuple(outs)
```

```python
import functools

import jax
import jax.numpy as jnp
from jax import lax
from jax.experimental import pallas as pl
from jax.experimental.pallas import tpu as pltpu

D_MODEL = 1024
N_HEADS = 8
HEAD_DIM = 64
A_WIDTH = N_HEADS * HEAD_DIM
NOPE_DIM = 64
ROPE_DIM = 32
QK_DIM = NOPE_DIM + ROPE_DIM
V_DIM = 64
Q_RANK = 384
KV_RANK = 256
DILATIONS = (1, 4, 16)
BAND = 64
N_EXPERTS = 32
TOP_K = 4
D_FF = 1024
SWIGLU_LIMIT = 7.0
SWIGLU_ALPHA = 1.702
ROPE_THETA = 10000.0
NORM_EPS = 1e-5
NEG_INF = -1e30

LANES = 128
IN_SPLITS = (0, 512, 1024, 1536, 1920, 2176, 2304)

PROJ_TM = 512
ATTN_TQ = 512
ATTN_TK = 256
ATTN_SCORE_BYTES = 24 * 1024 * 1024
DIL_QB = 128
DIL_UNROLL = 16
POST_TM = 512
DISPATCH_TM = 512
DMA_UNROLL = 4
FFN_TM = 512
FFN_SUB = 2
FFN_CAST_ROWS = 256
COMBINE_TM = 256

F32 = jnp.float32
BF16 = jnp.bfloat16


def _vmem(mib):
    return mib * 1024 * 1024


def _rms(x, g):
    return x * lax.rsqrt(jnp.mean(x * x, axis=-1, keepdims=True) + NORM_EPS) * g


CHUNKS = D_MODEL // LANES


def _load_token_tiles(ref, n, first=0):
    rows = lambda c: pl.ds(first * CHUNKS + c, n, stride=CHUNKS)
    return jnp.concatenate([ref[rows(c), :] for c in range(CHUNKS)], axis=1)


def _store_token_tiles(ref, x, first=0):
    for c in range(CHUNKS):
        ref[pl.ds(first * CHUNKS + c, x.shape[0], stride=CHUNKS), :] = x[:, c * LANES:(c + 1) * LANES]


def _proj_kernel(x_ref, g_ref, win_ref, qn_ref, wuq_ref, kvn_ref, wukv_ref,
                 cosa_ref, sina_ref, cosb_ref, sinb_ref,
                 qa_ref, ka_ref, va_ref, qb_ref, kb_ref, vbt_ref):
    tm = x_ref.shape[0]
    hb = _rms(x_ref[...], g_ref[...]).astype(BF16)

    def mm(g):
        return jnp.dot(hb, win_ref[:, IN_SPLITS[g]:IN_SPLITS[g + 1]], preferred_element_type=F32)

    lane_a = lax.broadcasted_iota(jnp.int32, (tm, A_WIDTH), 1)
    first_half = (lane_a % HEAD_DIM) < (HEAD_DIM // 2)
    cosa = jnp.tile(cosa_ref[...], (1, A_WIDTH // LANES))
    sina = jnp.tile(sina_ref[...], (1, A_WIDTH // LANES))

    def rope_a(t):
        sw = jnp.where(first_half, pltpu.roll(t, A_WIDTH - HEAD_DIM // 2, 1), pltpu.roll(t, HEAD_DIM // 2, 1))
        return t * cosa + sw * sina

    qa_ref[...] = rope_a(mm(0)) * (HEAD_DIM ** -0.5)
    ka_ref[...] = rope_a(mm(1))
    va_ref[...] = mm(2)

    def rope_b(t, cosb, sinb):
        w = t.shape[1]
        lane = lax.broadcasted_iota(jnp.int32, t.shape, 1) % LANES
        half = ROPE_DIM // 2
        sw = jnp.where(lane < NOPE_DIM + half, pltpu.roll(t, w - half, 1), pltpu.roll(t, half, 1))
        return t * cosb + sw * sinb

    cq = _rms(mm(3), qn_ref[...]).astype(BF16)
    qb = jnp.dot(cq, wuq_ref[...], preferred_element_type=F32)
    cosb8 = jnp.tile(cosb_ref[...], (1, N_HEADS))
    sinb8 = jnp.tile(sinb_ref[...], (1, N_HEADS))
    qb_ref[...] = (rope_b(qb, cosb8, sinb8) * (QK_DIM ** -0.5)).astype(BF16)

    ckv = _rms(mm(4), kvn_ref[...]).astype(BF16)
    kv = jnp.dot(ckv, wukv_ref[...], preferred_element_type=F32)
    kpe = rope_b(mm(5), cosb_ref[...], sinb_ref[...])
    kb_ref[...] = (kv[:, :N_HEADS * LANES] + jnp.tile(kpe, (1, N_HEADS))).astype(BF16)
    lane_v = lax.broadcasted_iota(jnp.int32, (tm, N_HEADS * LANES), 1) % LANES
    vbt_ref[0] = jnp.where(lane_v == V_DIM, 1.0, kv[:, N_HEADS * LANES:]).T.astype(BF16)


def _proj(x2d, seq, wts, tables):
    t = x2d.shape[0]
    tm = PROJ_TM
    n_seq_tiles = seq // tm
    cosa, sina, cosb, sinb = tables
    full = lambda a: pl.BlockSpec(a.shape, lambda i: (0,) * a.ndim)
    tab = pl.BlockSpec((tm, LANES), lambda i: (i % n_seq_tiles, 0))
    row = lambda w: pl.BlockSpec((tm, w), lambda i: (i, 0))
    outs = [A_WIDTH, A_WIDTH, A_WIDTH, N_HEADS * LANES, N_HEADS * LANES]
    vt_spec = pl.BlockSpec((1, N_HEADS * LANES, tm), lambda i: (i // n_seq_tiles, 0, i % n_seq_tiles))
    vt_shape = jax.ShapeDtypeStruct((t // seq, N_HEADS * LANES, seq), BF16)
    return pl.pallas_call(
        _proj_kernel,
        grid=(t // tm,),
        in_specs=[row(D_MODEL), full(wts["attn_norm"]), full(wts["w_in"]), full(wts["q_norm"]), full(wts["w_uq"]),
                  full(wts["kv_norm"]), full(wts["w_ukv"]), tab, tab, tab, tab],
        out_specs=[row(w) for w in outs] + [vt_spec],
        out_shape=[jax.ShapeDtypeStruct((t, w), F32 if j < 3 else BF16) for j, w in enumerate(outs)] + [vt_shape],
        compiler_params=pltpu.CompilerParams(dimension_semantics=("arbitrary",), vmem_limit_bytes=_vmem(56)),
        name="proj",
    )(x2d, wts["attn_norm"], wts["w_in"], wts["q_norm"], wts["w_uq"], wts["kv_norm"], wts["w_ukv"],
      cosa, sina, cosb, sinb)


def _attn_b_kernel(q_ref, k_ref, vt_ref, o_ref, st_ref, p_ref, *, seq):
    tq = q_ref.shape[1]
    n_heads = st_ref.shape[0]
    chunks = [slice(i * ATTN_TK, (i + 1) * ATTN_TK) for i in range(seq // ATTN_TK)]
    outs = []
    for j in range(n_heads):
        lanes = slice(LANES * j, LANES * (j + 1))
        st_ref[j] = lax.dot_general(k_ref[0, :, lanes], q_ref[0, :, lanes], (((1,), (1,)), ((), ())),
                                    preferred_element_type=F32)
    for j in range(n_heads):
        lanes = slice(LANES * j, LANES * (j + 1))
        m = jnp.full((1, tq), NEG_INF, F32)
        for rows in chunks:
            m = jnp.maximum(m, jnp.max(st_ref[j, rows, :], axis=0, keepdims=True))
        for rows in chunks:
            p_ref[j, rows, :] = jnp.exp(st_ref[j, rows, :] - m).astype(BF16)
        acc = jnp.dot(vt_ref[0, lanes, :], p_ref[j], preferred_element_type=F32)
        outs.append((acc / acc[V_DIM:V_DIM + 1, :])[:V_DIM])
    o_ref[0] = jnp.concatenate(outs, axis=0).T.astype(BF16)


def _attn_b(qb, kb, vbt, batch, seq):
    q3 = qb.reshape(batch, seq, N_HEADS * LANES)
    k3 = kb.reshape(batch, seq, N_HEADS * LANES)
    tq = ATTN_TQ
    heads = max(2, min(N_HEADS, ATTN_SCORE_BYTES // (6 * seq * tq)))
    width = heads * LANES
    out = pl.pallas_call(
        functools.partial(_attn_b_kernel, seq=seq),
        grid=(batch, N_HEADS // heads, seq // tq),
        in_specs=[pl.BlockSpec((1, tq, width), lambda b, h, i: (b, i, h)),
                  pl.BlockSpec((1, seq, width), lambda b, h, i: (b, 0, h)),
                  pl.BlockSpec((1, width, seq), lambda b, h, i: (b, h, 0))],
        out_specs=pl.BlockSpec((1, tq, heads * V_DIM), lambda b, h, i: (b, i, h)),
        out_shape=jax.ShapeDtypeStruct((batch, seq, N_HEADS * V_DIM), BF16),
        scratch_shapes=[pltpu.VMEM((heads, seq, tq), F32), pltpu.VMEM((heads, seq, tq), BF16)],
        compiler_params=pltpu.CompilerParams(dimension_semantics=("arbitrary",) * 3, vmem_limit_bytes=_vmem(56)),
        name="attn_b",
    )(q3, k3, vbt)
    return out.reshape(batch * seq, N_HEADS * V_DIM)


def _dil_all_kernel(bias_ref, q_ref, k_ref, v_ref, o_ref, acc_ref, m_ref, l_ref, *, seq):
    qb_rows = DIL_QB
    lane = lax.broadcasted_iota(jnp.int32, (qb_rows, LANES), 1)
    low = lane < HEAD_DIM
    for dil in DILATIONS:
        length = seq // dil
        win = min(2 * qb_rows, length)
        nblk = length // qb_rows

        def body(n, carry, dil=dil, length=length, win=win, nblk=nblk):
            res = n // nblk
            r0 = (n % nblk) * qb_rows
            start = jnp.clip(r0 - BAND, 0, length - win)
            if dil == 1:
                qrows = pl.ds(pl.multiple_of(r0, qb_rows), qb_rows)
                krows = pl.ds(pl.multiple_of(start, BAND), win)
            else:
                qrows = pl.ds(res + dil * r0, qb_rows, stride=dil)
                krows = pl.ds(res + dil * start, win, stride=dil)
            case = jnp.where(r0 == 0, 0, jnp.where(r0 == length - qb_rows, 2, 1))
            bias = bias_ref[case, :, :win]
            qp = q_ref[qrows, :]
            zero = jnp.zeros_like(qp)
            qq = jnp.concatenate([jnp.where(low, qp, zero), jnp.where(low, zero, qp)], axis=0).astype(BF16)
            kw = k_ref[krows, :].astype(BF16)
            vw = v_ref[krows, :].astype(BF16)
            s = lax.dot_general(qq, kw, (((1,), (1,)), ((), ())), preferred_element_type=F32) + bias
            m = jnp.max(s, axis=-1, keepdims=True)
            p = jnp.exp(s - m)
            l = jnp.sum(p, axis=-1, keepdims=True)
            pv = jnp.dot(p.astype(BF16), vw, preferred_element_type=F32)
            pick =lambda t: jnp.where(low, t[:qb_rows], t[qb_rows:])
            m_new, l_new, pv_new = pick(m), pick(l), pick(pv)
            if dil == DILATIONS[0]:
                m_ref[qrows, :] = m_new
                l_ref[qrows, :] = l_new
                acc_ref[qrows, :] = pv_new
            else:
                m_old = m_ref[qrows, :]
                top = jnp.maximum(m_old, m_new)
                a = jnp.exp(m_old - top)
                b = jnp.exp(m_new - top)
                m_ref[qrows, :] = top
                l_ref[qrows, :] = a * l_ref[qrows, :] + b * l_new
                acc_ref[qrows, :] = a * acc_ref[qrows, :] + b * pv_new
            return carry

        lax.fori_loop(0, dil * nblk, body, 0, unroll=DIL_UNROLL)
    for c in range(seq // 512):
        rows = slice(c * 512, (c + 1) * 512)
        o_ref[rows, :] = (acc_ref[rows, :] / l_ref[rows, :]).astype(BF16)


def _band_bias():
    q = jnp.arange(2 * DIL_QB, dtype=jnp.int32)[None, :, None] % DIL_QB
    k = jnp.arange(2 * DIL_QB, dtype=jnp.int32)[None, None, :]
    delta = jnp.array([0, -BAND, -DIL_QB], jnp.int32)[:, None, None]
    return jnp.where(jnp.abs(k + delta - q) <= BAND, 0.0, NEG_INF).astype(F32)


def _dilated_all(qa, ka, va, batch, seq):
    view = lambda a: a.reshape(batch, seq, A_WIDTH)
    spec = pl.BlockSpec((None, seq, LANES), lambda b, h: (b, 0, h))
    bias = _band_bias()
    out = pl.pallas_call(
        functools.partial(_dil_all_kernel, seq=seq),
        grid=(batch, A_WIDTH // LANES),
        in_specs=[pl.BlockSpec(bias.shape, lambda b, h: (0, 0, 0)), spec, spec, spec],
        out_specs=spec,
        out_shape=jax.ShapeDtypeStruct((batch, seq, A_WIDTH), BF16),
        scratch_shapes=[pltpu.VMEM((seq, LANES), F32)] * 3,
        compiler_params=pltpu.CompilerParams(dimension_semantics=("arbitrary",) * 2, vmem_limit_bytes=_vmem(48)),
        name="dilated",
    )(bias, view(qa), view(ka), view(va))
    return out.reshape(batch * seq, A_WIDTH)


def _post_kernel(xp_ref, oap_ref, obp_ref, xs_ref, oas_ref, obs_ref, wo_ref, g_ref, rwh_ref, rwl_ref, rb_ref,
                 x1_ref, hn_ref, route_ref, routet_ref, cout_ref, carry_ref, *, prompt_tiles):
    tm = xp_ref.shape[0]
    is_prompt = pl.program_id(0) < prompt_tiles

    @pl.when(pl.program_id(0) == 0)
    def _():
        carry_ref[...] = jnp.zeros_like(carry_ref)

    oa = jnp.where(is_prompt, oap_ref[...], oas_ref[...])
    ob = jnp.where(is_prompt, obp_ref[...], obs_ref[...])
    attn = jnp.dot(oa, wo_ref[:A_WIDTH, :], preferred_element_type=F32)
    attn += jnp.dot(ob, wo_ref[A_WIDTH:, :], preferred_element_type=F32)
    x1 = jnp.where(is_prompt, xp_ref[...], xs_ref[...]) + attn
    x1_ref[...] = x1
    hn = _rms(x1, g_ref[...])
    _store_token_tiles(hn_ref, hn)

    hi = hn.astype(BF16)
    lo = (hn - hi.astype(F32)).astype(BF16)
    logits = jnp.dot(hi, rwh_ref[...], preferred_element_type=F32)
    logits += jnp.dot(lo, rwh_ref[...], preferred_element_type=F32)
    logits += jnp.dot(hi, rwl_ref[...], preferred_element_type=F32)
    logits += rb_ref[...]

    lane = lax.broadcasted_iota(jnp.int32, (tm, LANES), 1)
    work = logits
    vals, sels = [], []
    for _ in range(TOP_K):
        mx = jnp.max(work, axis=-1, keepdims=True)
        first = jnp.min(jnp.where(work == mx, lane, LANES), axis=-1, keepdims=True)
        sel = lane == first
        work = jnp.where(sel, -jnp.inf, work)
        vals.append(mx)
        sels.append(sel)
    exps = [jnp.exp(v - vals[0]) for v in vals]
    den = exps[0] + exps[1] + exps[2] + exps[3]

    sel_all = (sels[0] | sels[1] | sels[2] | sels[3]).astype(F32)
    tri = (lax.broadcasted_iota(jnp.int32, (tm, tm), 1) < lax.broadcasted_iota(jnp.int32, (tm, tm), 0)).astype(BF16)
    before = jnp.dot(tri, sel_all.astype(BF16), preferred_element_type=F32) + carry_ref[0:1, :]
    carry_ref[0:1, :] = carry_ref[0:1, :] + jnp.sum(sel_all, axis=0, keepdims=True)
    cout_ref[...] = carry_ref[...]

    lane_f = lane.astype(F32)
    route = jnp.zeros((tm, LANES), F32)
    for k in range(TOP_K):
        idx = jnp.sum(jnp.where(sels[k], lane_f, 0.0), axis=-1, keepdims=True)
        rank = jnp.sum(jnp.where(sels[k], before, 0.0), axis=-1, keepdims=True)
        route = jnp.where(lane == k, idx, route)
        route = jnp.where(lane == TOP_K + k, rank, route)
        route = jnp.where(lane == 2 * TOP_K + k, exps[k] / den, route)
    route_ref[...] = route
    routet_ref[...] = route.T[:16, :]


def _post(prompt, sample, wts):
    tm = POST_TM
    n_p = prompt[0].shape[0] // tm
    n_s = sample[0].shape[0] // tm
    total = (n_p + n_s) * tm
    full = lambda a: pl.BlockSpec(a.shape, lambda i: (0,) * a.ndim)
    row_p = lambda w: pl.BlockSpec((tm, w), lambda i: (jnp.minimum(i, n_p - 1), 0))
    row_s = lambda w: pl.BlockSpec((tm, w), lambda i: (jnp.maximum(i - n_p, 0), 0))
    row = lambda w: pl.BlockSpec((tm, w), lambda i: (i, 0))
    widths = (D_MODEL, A_WIDTH, A_WIDTH)
    consts = [wts["w_o"], wts["ffn_norm"], wts["router_hi"], wts["router_lo"], wts["router_b"]]
    out_shape = [jax.ShapeDtypeStruct((total, D_MODEL), F32), jax.ShapeDtypeStruct((total * CHUNKS, LANES), F32),
                 jax.ShapeDtypeStruct((total, LANES), F32), jax.ShapeDtypeStruct((16, total), F32),
                 jax.ShapeDtypeStruct((8, LANES), F32)]
    out_specs = [row(D_MODEL), pl.BlockSpec((tm * CHUNKS, LANES), lambda i: (i, 0)), row(LANES),
                 pl.BlockSpec((16, tm), lambda i: (0, i)), pl.BlockSpec((8, LANES), lambda i: (0, 0))]
    return pl.pallas_call(
        functools.partial(_post_kernel, prompt_tiles=n_p),
        grid=(n_p + n_s,),
        in_specs=[row_p(w) for w in widths] + [row_s(w) for w in widths] + [full(c) for c in consts],
        out_specs=out_specs,
        out_shape=out_shape,
        scratch_shapes=[pltpu.VMEM((8, LANES), F32)],
        compiler_params=pltpu.CompilerParams(dimension_semantics=("arbitrary",), vmem_limit_bytes=_vmem(56)),
        name="post",
    )(*prompt, *sample, *consts)


def _dispatch_kernel(pos_ref, hn_ref, xs_ref, sem):
    tm = DISPATCH_TM

    def copy(t, slot):
        return pltpu.make_async_copy(hn_ref.at[t], xs_ref.at[slot], sem)

    def issue(t, c):
        for k in range(TOP_K):
            copy(t, pos_ref[0, 0, k * tm + t]).start(priority=k % 2)
        return c

    def drain(t, c):
        for k in range(TOP_K):
            copy(0, 0).wait()
        return c

    lax.fori_loop(0, tm, issue, 0, unroll=DMA_UNROLL)
    lax.fori_loop(0, tm, drain, 0, unroll=DMA_UNROLL)


def _dispatch(pos_blocks, hn, rows):
    tm = DISPATCH_TM
    hn3 = hn.reshape(-1, CHUNKS, LANES)
    return pl.pallas_call(
        _dispatch_kernel,
        grid=(hn3.shape[0] // tm,),
        in_specs=[pl.BlockSpec((1, 1, TOP_K * tm), lambda i: (i, 0, 0), memory_space=pltpu.SMEM),
                  pl.BlockSpec((tm, CHUNKS, LANES), lambda i: (i, 0, 0))],
        out_specs=pl.BlockSpec(memory_space=pl.ANY),
        out_shape=jax.ShapeDtypeStruct((rows, CHUNKS, LANES), F32),
        scratch_shapes=[pltpu.SemaphoreType.DMA(())],
        compiler_params=pltpu.CompilerParams(dimension_semantics=("arbitrary",)),
        name="dispatch",
    )(pos_blocks, hn3)


def _combine_kernel(pos_ref, pos_next_ref, ys_ref, x1_ref, route_ref, g_ref, o_ref, ybuf, sem, *, steps):
    tm = COMBINE_TM
    step = pl.program_id(0)
    cur = step % 2

    def copy(slot, buf, k, t):
        dst = ybuf.at[buf, k, pl.ds(pl.multiple_of(t * CHUNKS, CHUNKS), CHUNKS)]
        return pltpu.make_async_copy(ys_ref.at[slot], dst, sem.at[buf])

    def fetch(index_ref, buf):
        def issue(t, c):
            for k in range(TOP_K):
                copy(index_ref[0, 0, k * tm + t], buf, k, t).start(priority=k % 2)
            return c

        lax.fori_loop(0, tm, issue, 0, unroll=DMA_UNROLL)

    @pl.when(step == 0)
    def _():
        fetch(pos_ref, 0)

    @pl.when(step + 1 < steps)
    def _():
        fetch(pos_next_ref, 1 - cur)

    def drain(t, c):
        for k in range(TOP_K):
            copy(0, cur, k, 0).wait()
        return c

    lax.fori_loop(0, tm, drain, 0, unroll=DMA_UNROLL)
    route = route_ref[...]
    y = x1_ref[...]
    for k in range(TOP_K):
        y = y + route[:, 2 * TOP_K + k:2 * TOP_K + k + 1] * _load_token_tiles(ybuf.at[cur, k], tm)
    o_ref[...] = _rms(y, g_ref[...])


def _combine(pos_blocks, ys, x1, route, final_norm, row_off, rows):
    tm = COMBINE_TM
    off = row_off // tm
    last = off + rows // tm - 1
    pos_spec = lambda ahead: pl.BlockSpec((1, 1, TOP_K * tm), lambda i: (jnp.minimum(i + off + ahead, last), 0, 0),
                                          memory_space=pltpu.SMEM)
    return pl.pallas_call(
        functools.partial(_combine_kernel, steps=rows // tm),
        grid=(rows // tm,),
        in_specs=[pos_spec(0), pos_spec(1),
                  pl.BlockSpec(memory_space=pl.ANY),
                  pl.BlockSpec((tm, D_MODEL), lambda i: (i + off, 0)),
                  pl.BlockSpec((tm, LANES), lambda i: (i + off, 0)),
                  pl.BlockSpec((1, D_MODEL), lambda i: (0, 0))],
        out_specs=pl.BlockSpec((tm, D_MODEL), lambda i: (i, 0)),
        out_shape=jax.ShapeDtypeStruct((rows, D_MODEL), F32),
        scratch_shapes=[pltpu.VMEM((2, TOP_K, tm * CHUNKS, LANES), F32), pltpu.SemaphoreType.DMA((2,))],
        compiler_params=pltpu.CompilerParams(dimension_semantics=("arbitrary",), vmem_limit_bytes=_vmem(32)),
        name="combine",
    )(pos_blocks, pos_blocks, ys, x1, route, final_norm)


def _ffn_kernel(tile_ref, expert_ref, npairs_ref, x_ref, wgu_ref, bgu_ref, wd_ref, bd_ref, y_ref, wgu_bf, wd_bf):
    step = pl.program_id(0)
    n = FFN_TM // FFN_SUB

    @pl.when((step == 0) | (expert_ref[step] != expert_ref[jnp.maximum(step - 1, 0)]))
    def _():
        for r in range(0, D_MODEL, FFN_CAST_ROWS):
            rows = slice(r, r + FFN_CAST_ROWS)
            wgu_bf[rows, :] = wgu_ref[0, rows, :].astype(BF16)
            wd_bf[rows, :] = wd_ref[0, rows, :].astype(BF16)

    @pl.when(step >= npairs_ref[0])
    def _():
        y_ref[...] = jnp.zeros_like(y_ref)

    @pl.when(step < npairs_ref[0])
    def _():
        for h in range(FFN_SUB):
            xb = _load_token_tiles(x_ref, n, h * n).astype(BF16)
            gate = jnp.dot(xb, wgu_bf[:, :D_FF], preferred_element_type=F32) + bgu_ref[0, :, :D_FF]
            up = jnp.dot(xb, wgu_bf[:, D_FF:], preferred_element_type=F32) + bgu_ref[0, :, D_FF:]
            gate = jnp.minimum(gate, SWIGLU_LIMIT)
            up = jnp.clip(up, -SWIGLU_LIMIT, SWIGLU_LIMIT)
            act = (up + 1.0) * gate * (1.0 / (1.0 + jnp.exp(-SWIGLU_ALPHA * gate)))
            res = jnp.dot(act.astype(BF16), wd_bf[...], preferred_element_type=F32) + bd_ref[0]
            _store_token_tiles(y_ref, res, h * n)


def _ffn(pairs, xs, wts):
    tm = FFN_TM
    pair_tile, pair_expert, n_pairs = pairs
    slots = pair_tile.shape[0]
    x_map = lambda i, pt, pe, n: (pt[i], 0)
    w_map = lambda i, pt, pe, n: (pe[i], 0, 0)
    grid_spec = pltpu.PrefetchScalarGridSpec(
        num_scalar_prefetch=3,
        grid=(slots,),
        in_specs=[pl.BlockSpec((tm * CHUNKS, LANES), x_map),
                  pl.BlockSpec((1, D_MODEL, 2 * D_FF), w_map),
                  pl.BlockSpec((1, 1, 2 * D_FF), w_map),
                  pl.BlockSpec((1, D_FF, D_MODEL), w_map),
                  pl.BlockSpec((1, 1, D_MODEL), w_map)],
        out_specs=pl.BlockSpec((tm * CHUNKS, LANES), lambda i, pt, pe, n: (i, 0)),
        scratch_shapes=[pltpu.VMEM((D_MODEL, 2 * D_FF), BF16), pltpu.VMEM((D_FF, D_MODEL), BF16)],
    )
    ys = pl.pallas_call(
        _ffn_kernel,
        grid_spec=grid_spec,
        out_shape=jax.ShapeDtypeStruct((slots * tm * CHUNKS, LANES), F32),
        compiler_params=pltpu.CompilerParams(dimension_semantics=("arbitrary",), vmem_limit_bytes=_vmem(58)),
        name="ffn",
    )(pair_tile, pair_expert, n_pairs, xs.reshape(-1, LANES), wts["w_gate_up"], wts["b_gate_up"],
      wts["w_down"], wts["b_down"])
    return ys.reshape(slots * tm, CHUNKS, LANES)


def _rope_tables(seq):
    pos = jnp.arange(seq, dtype=F32)[:, None]

    def cs(dim):
        inv = 1.0 / (ROPE_THETA ** (jnp.arange(0, dim, 2, dtype=F32) / dim))
        ang = pos * inv[None, :]
        return jnp.cos(ang), jnp.sin(ang)

    ca, sa = cs(HEAD_DIM)
    cb, sb = cs(ROPE_DIM)
    one = jnp.ones((seq, NOPE_DIM), F32)
    pad1 = jnp.ones((seq, LANES - QK_DIM), F32)
    cosa = jnp.concatenate([ca, ca, ca, ca], axis=1)
    sina = jnp.concatenate([-sa, sa, -sa, sa], axis=1)
    cosb = jnp.concatenate([one, cb, cb, pad1], axis=1)
    sinb = jnp.concatenate([0.0 * one, -sb, sb, 0.0 * pad1], axis=1)
    return cosa, sina, cosb, sinb


def _prep_weights(attn_norm, w_in, q_norm, w_uq, kv_norm, w_ukv, w_o, ffn_norm, router_w, router_b,
                  w_gate_up, b_gate_up, w_down, b_down):
    w = w_in[0]
    zeros = lambda n: jnp.zeros((D_MODEL, n), F32)
    krope = jnp.concatenate([zeros(NOPE_DIM), w[:, 2176:], zeros(LANES - QK_DIM)], axis=1)
    w_in_p = jnp.concatenate([w[:, :2176], krope], axis=1).astype(BF16)
    uq = w_uq[0].reshape(Q_RANK, N_HEADS, QK_DIM)
    uq = jnp.pad(uq, ((0, 0), (0, 0), (0, LANES - QK_DIM))).reshape(Q_RANK, N_HEADS * LANES).astype(BF16)
    ukv = w_ukv[0].reshape(KV_RANK, N_HEADS, NOPE_DIM + V_DIM)
    pad = lambda a: jnp.pad(a, ((0, 0), (0, 0), (0, LANES - a.shape[2]))).reshape(KV_RANK, N_HEADS * LANES)
    ukv = jnp.concatenate([pad(ukv[:, :, :NOPE_DIM]), pad(ukv[:, :, NOPE_DIM:])], axis=1).astype(BF16)
    rw = jnp.pad(router_w[0], ((0, 0), (0, LANES - N_EXPERTS)))
    rw_hi = rw.astype(BF16)
    rw_lo = (rw - rw_hi.astype(F32)).astype(BF16)
    rb = jnp.concatenate([router_b[0], jnp.full((LANES - N_EXPERTS,), NEG_INF, F32)])[None, :]
    return {
        "attn_norm": attn_norm[0][None, :], "w_in": w_in_p, "q_norm": q_norm[0][None, :], "w_uq": uq,
        "kv_norm": kv_norm[0][None, :], "w_ukv": ukv, "w_o": w_o[0].astype(BF16), "ffn_norm": ffn_norm[0][None, :],
        "router_hi": rw_hi, "router_lo": rw_lo, "router_b": rb,
        "w_gate_up": w_gate_up[0], "b_gate_up": b_gate_up[0][:, None, :],
        "w_down": w_down[0], "b_down": b_down[0][:, None, :],
    }


def _routing(route_t, counts, total):
    i32 = jnp.int32
    idx = route_t[0:TOP_K].astype(i32)
    rank = route_t[TOP_K:2 * TOP_K].astype(i32)
    cnt = counts[0, :N_EXPERTS].astype(i32)
    seg_end = jnp.cumsum(cnt)
    seg_start = seg_end - cnt
    experts = jnp.arange(N_EXPERTS, dtype=i32)
    per_expert = lambda table: jnp.sum(jnp.where(idx[None] == experts[:, None, None], table[:, None, None], 0), axis=0)
    pos = rank + per_expert(seg_start)

    first_tile = seg_start // FFN_TM
    n_per = jnp.where(cnt > 0, (seg_end - 1) // FFN_TM - first_tile + 1, 0)
    pair_end = jnp.cumsum(n_per)
    n_pairs = pair_end[-1]
    pos_out = pos + per_expert((pair_end - n_per - first_tile) * FFN_TM)
    slots = total * TOP_K // FFN_TM + N_EXPERTS
    j = jnp.minimum(jnp.arange(slots, dtype=i32), n_pairs - 1)
    onehot = (jnp.sum((j[:, None] >= pair_end[None, :]).astype(i32), axis=1)[:, None] == experts[None, :]).astype(i32)
    pick = lambda table: jnp.sum(onehot * table[None, :], axis=1)
    pair_expert = pick(experts)
    pair_tile = pick(first_tile) + j - pick(pair_end - n_per)

    def blocks(p, tm):
        return p.reshape(TOP_K, total // tm, tm).transpose(1, 0, 2).reshape(total // tm, 1, TOP_K * tm)

    return blocks(pos, DISPATCH_TM), blocks(pos_out, COMBINE_TM), (pair_tile, pair_expert, n_pairs[None].astype(i32))


def _mixers(x, wts):
    batch, seq, _ = x.shape
    x2d = x.reshape(batch * seq, D_MODEL)
    qa, ka, va, qb, kb, vbt = _proj(x2d, seq, wts, _rope_tables(seq))
    oa = _dilated_all(qa, ka, va, batch, seq)
    ob = _attn_b(qb, kb, vbt, batch, seq)
    return x2d, oa, ob


def kernel(x_prompt, x_sample, attn_norm, w_in, q_norm, w_uq, kv_norm, w_ukv, w_o, ffn_norm, router_w, router_b,
           w_gate_up, b_gate_up, w_down, b_down, final_norm):
    wts = _prep_weights(attn_norm, w_in, q_norm, w_uq, kv_norm, w_ukv, w_o, ffn_norm, router_w, router_b,
                        w_gate_up, b_gate_up, w_down, b_down)
    sets = [x_prompt, x_sample]
    rows = [x.shape[0] * x.shape[1] for x in sets]
    total = sum(rows)
    x1, hn, route, route_t, counts = _post(_mixers(x_prompt, wts), _mixers(x_sample, wts), wts)
    pos_dispatch, pos_combine, pairs = _routing(route_t, counts, total)
    xs = _dispatch(pos_dispatch, hn, total * TOP_K)
    ys = _ffn(pairs, xs, wts)
    outs = []
    off = 0
    fnorm = final_norm[None, :]
    for x, n in zip(sets, rows):
        outs.append(_combine(pos_combine, ys, x1, route, fnorm, off, n).reshape(x.shape))
        off += n
    return tuple(outs)
```

```python
import functools

import jax
import jax.numpy as jnp
from jax import lax
from jax.experimental import pallas as pl
from jax.experimental.pallas import tpu as pltpu

D_MODEL = 1024
N_HEADS = 8
HEAD_DIM = 64
A_WIDTH = N_HEADS * HEAD_DIM
NOPE_DIM = 64
ROPE_DIM = 32
QK_DIM = NOPE_DIM + ROPE_DIM
V_DIM = 64
Q_RANK = 384
KV_RANK = 256
DILATIONS = (1, 4, 16)
BAND = 64
N_EXPERTS = 32
TOP_K = 4
D_FF = 1024
SWIGLU_LIMIT = 7.0
SWIGLU_ALPHA = 1.702
ROPE_THETA = 10000.0
NORM_EPS = 1e-5
NEG_INF = -1e30

LANES = 128
IN_SPLITS = (0, 512, 1024, 1536, 1920, 2176, 2304)

PROJ_TM = 512
ATTN_TQ = 512
ATTN_TK = 256
ATTN_SCORE_BYTES = 24 * 1024 * 1024
DIL_QB = 128
DIL_UNROLL = 16
POST_TM = 512
DISPATCH_TM = 1024
DMA_UNROLL = 4
FFN_TM = 512
FFN_SUB = 2
FFN_CAST_ROWS = 256
COMBINE_TM = 256

F32 = jnp.float32
BF16 = jnp.bfloat16


def _vmem(mib):
    return mib * 1024 * 1024


def _rms(x, g):
    return x * lax.rsqrt(jnp.mean(x * x, axis=-1, keepdims=True) + NORM_EPS) * g


CHUNKS = D_MODEL // LANES


def _load_token_tiles(ref, n, first=0):
    rows = lambda c: pl.ds(first * CHUNKS + c, n, stride=CHUNKS)
    return jnp.concatenate([ref[rows(c), :] for c in range(CHUNKS)], axis=1)


def _store_token_tiles(ref, x, first=0):
    for c in range(CHUNKS):
        ref[pl.ds(first * CHUNKS + c, x.shape[0], stride=CHUNKS), :] = x[:, c * LANES:(c + 1) * LANES]


def _proj_kernel(x_ref, g_ref, win_ref, qn_ref, wuq_ref, kvn_ref, wukv_ref,
                 cosa_ref, sina_ref, cosb_ref, sinb_ref,
                 qa_ref, ka_ref, va_ref, qb_ref, kb_ref, vbt_ref):
    tm = x_ref.shape[0]
    hb = _rms(x_ref[...], g_ref[...]).astype(BF16)

    def mm(g):
        return jnp.dot(hb, win_ref[:, IN_SPLITS[g]:IN_SPLITS[g + 1]], preferred_element_type=F32)

    lane_a = lax.broadcasted_iota(jnp.int32, (tm, A_WIDTH), 1)
    first_half = (lane_a % HEAD_DIM) < (HEAD_DIM // 2)
    cosa = jnp.tile(cosa_ref[...], (1, A_WIDTH // LANES))
    sina = jnp.tile(sina_ref[...], (1, A_WIDTH // LANES))

    def rope_a(t):
        sw = jnp.where(first_half, pltpu.roll(t, A_WIDTH - HEAD_DIM // 2, 1), pltpu.roll(t, HEAD_DIM // 2, 1))
        return t * cosa + sw * sina

    qa_ref[...] = rope_a(mm(0)) * (HEAD_DIM ** -0.5)
    ka_ref[...] = rope_a(mm(1))
    va_ref[...] = mm(2)

    def rope_b(t, cosb, sinb):
        w = t.shape[1]
        lane = lax.broadcasted_iota(jnp.int32, t.shape, 1) % LANES
        half = ROPE_DIM // 2
        sw = jnp.where(lane < NOPE_DIM + half, pltpu.roll(t, w - half, 1), pltpu.roll(t, half, 1))
        return t * cosb + sw * sinb

    cq = _rms(mm(3), qn_ref[...]).astype(BF16)
    qb = jnp.dot(cq, wuq_ref[...], preferred_element_type=F32)
    cosb8 = jnp.tile(cosb_ref[...], (1, N_HEADS))
    sinb8 = jnp.tile(sinb_ref[...], (1, N_HEADS))
    qb_ref[...] = (rope_b(qb, cosb8, sinb8) * (QK_DIM ** -0.5)).astype(BF16)

    ckv = _rms(mm(4), kvn_ref[...]).astype(BF16)
    kv = jnp.dot(ckv, wukv_ref[...], preferred_element_type=F32)
    kpe = rope_b(mm(5), cosb_ref[...], sinb_ref[...])
    kb_ref[...] = (kv[:, :N_HEADS * LANES] + jnp.tile(kpe, (1, N_HEADS))).astype(BF16)
    lane_v = lax.broadcasted_iota(jnp.int32, (tm, N_HEADS * LANES), 1) % LANES
    vbt_ref[0] = jnp.where(lane_v == V_DIM, 1.0, kv[:, N_HEADS * LANES:]).T.astype(BF16)


def _proj(x2d, seq, wts, tables):
    t = x2d.shape[0]
    tm = PROJ_TM
    n_seq_tiles = seq // tm
    cosa, sina, cosb, sinb = tables
    full = lambda a: pl.BlockSpec(a.shape, lambda i: (0,) * a.ndim)
    tab = pl.BlockSpec((tm, LANES), lambda i: (i % n_seq_tiles, 0))
    row = lambda w: pl.BlockSpec((tm, w), lambda i: (i, 0))
    outs = [A_WIDTH, A_WIDTH, A_WIDTH, N_HEADS * LANES, N_HEADS * LANES]
    vt_spec = pl.BlockSpec((1, N_HEADS * LANES, tm), lambda i: (i // n_seq_tiles, 0, i % n_seq_tiles))
    vt_shape = jax.ShapeDtypeStruct((t // seq, N_HEADS * LANES, seq), BF16)
    return pl.pallas_call(
        _proj_kernel,
        grid=(t // tm,),
        in_specs=[row(D_MODEL), full(wts["attn_norm"]), full(wts["w_in"]), full(wts["q_norm"]), full(wts["w_uq"]),
                  full(wts["kv_norm"]), full(wts["w_ukv"]), tab, tab, tab, tab],
        out_specs=[row(w) for w in outs] + [vt_spec],
        out_shape=[jax.ShapeDtypeStruct((t, w), F32 if j < 3 else BF16) for j, w in enumerate(outs)] + [vt_shape],
        compiler_params=pltpu.CompilerParams(dimension_semantics=("arbitrary",), vmem_limit_bytes=_vmem(56)),
        name="proj",
    )(x2d, wts["attn_norm"], wts["w_in"], wts["q_norm"], wts["w_uq"], wts["kv_norm"], wts["w_ukv"],
      cosa, sina, cosb, sinb)


def _attn_b_kernel(q_ref, k_ref, vt_ref, o_ref, st_ref, p_ref, *, seq):
    tq = q_ref.shape[1]
    n_heads = st_ref.shape[0]
    chunks = [slice(i * ATTN_TK, (i + 1) * ATTN_TK) for i in range(seq // ATTN_TK)]
    outs = []
    for j in range(n_heads):
        lanes = slice(LANES * j, LANES * (j + 1))
        st_ref[j] = lax.dot_general(k_ref[0, :, lanes], q_ref[0, :, lanes], (((1,), (1,)), ((), ())),
                                    preferred_element_type=F32)
    for j in range(n_heads):
        lanes = slice(LANES * j, LANES * (j + 1))
        m = jnp.full((1, tq), NEG_INF, F32)
        for rows in chunks:
            m = jnp.maximum(m, jnp.max(st_ref[j, rows, :], axis=0, keepdims=True))
        for rows in chunks:
            p_ref[j, rows, :] = jnp.exp(st_ref[j, rows, :] - m).astype(BF16)
        acc = jnp.dot(vt_ref[0, lanes, :], p_ref[j], preferred_element_type=F32)
        outs.append((acc / acc[V_DIM:V_DIM + 1, :])[:V_DIM])
    o_ref[0] = jnp.concatenate(outs, axis=0).T.astype(BF16)


def _attn_b(qb, kb, vbt, batch, seq):
    q3 = qb.reshape(batch, seq, N_HEADS * LANES)
    k3 = kb.reshape(batch, seq, N_HEADS * LANES)
    tq = ATTN_TQ
    heads = max(2, min(N_HEADS, ATTN_SCORE_BYTES // (6 * seq * tq)))
    width = heads * LANES
    out = pl.pallas_call(
        functools.partial(_attn_b_kernel, seq=seq),
        grid=(batch, N_HEADS // heads, seq // tq),
        in_specs=[pl.BlockSpec((1, tq, width), lambda b, h, i: (b, i, h)),
                  pl.BlockSpec((1, seq, width), lambda b, h, i: (b, 0, h)),
                  pl.BlockSpec((1, width, seq), lambda b, h, i: (b, h, 0))],
        out_specs=pl.BlockSpec((1, tq, heads * V_DIM), lambda b, h, i: (b, i, h)),
        out_shape=jax.ShapeDtypeStruct((batch, seq, N_HEADS * V_DIM), BF16),
        scratch_shapes=[pltpu.VMEM((heads, seq, tq), F32), pltpu.VMEM((heads, seq, tq), BF16)],
        compiler_params=pltpu.CompilerParams(dimension_semantics=("arbitrary",) * 3, vmem_limit_bytes=_vmem(56)),
        name="attn_b",
    )(q3, k3, vbt)
    return out.reshape(batch * seq, N_HEADS * V_DIM)


def _dil_all_kernel(bias_ref, q_ref, k_ref, v_ref, o_ref, acc_ref, m_ref, l_ref, *, seq):
    qb_rows = DIL_QB
    lane = lax.broadcasted_iota(jnp.int32, (qb_rows, LANES), 1)
    low = lane < HEAD_DIM
    for dil in DILATIONS:
        length = seq // dil
        win = min(2 * qb_rows, length)
        nblk = length // qb_rows

        def body(n, carry, dil=dil, length=length, win=win, nblk=nblk):
            res = n // nblk
            r0 = (n % nblk) * qb_rows
            start = jnp.clip(r0 - BAND, 0, length - win)
            if dil == 1:
                qrows = pl.ds(pl.multiple_of(r0, qb_rows), qb_rows)
                krows = pl.ds(pl.multiple_of(start, BAND), win)
            else:
                qrows = pl.ds(res + dil * r0, qb_rows, stride=dil)
                krows = pl.ds(res + dil * start, win, stride=dil)
            case = jnp.where(r0 == 0, 0, jnp.where(r0 == length - qb_rows, 2, 1))
            bias = bias_ref[case, :, :win]
            qp = q_ref[qrows, :]
            zero = jnp.zeros_like(qp)
            qq = jnp.concatenate([jnp.where(low, qp, zero), jnp.where(low, zero, qp)], axis=0).astype(BF16)
            kw = k_ref[krows, :].astype(BF16)
            vw = v_ref[krows, :].astype(BF16)
            s = lax.dot_general(qq, kw, (((1,), (1,)), ((), ())), preferred_element_type=F32) + bias
            m = jnp.max(s, axis=-1, keepdims=True)
            p = jnp.exp(s - m)
            l = jnp.sum(p, axis=-1, keepdims=True)
            pv = jnp.dot(p.astype(BF16), vw, preferred_element_type=F32)
            pick =lambda t: jnp.where(low, t[:qb_rows], t[qb_rows:])
            m_new, l_new, pv_new = pick(m), pick(l), pick(pv)
            if dil == DILATIONS[0]:
                m_ref[qrows, :] = m_new
                l_ref[qrows, :] = l_new
                acc_ref[qrows, :] = pv_new
            else:
                m_old = m_ref[qrows, :]
                top = jnp.maximum(m_old, m_new)
                a = jnp.exp(m_old - top)
                b = jnp.exp(m_new - top)
                m_ref[qrows, :] = top
                l_ref[qrows, :] = a * l_ref[qrows, :] + b * l_new
                acc_ref[qrows, :] = a * acc_ref[qrows, :] + b * pv_new
            return carry

        lax.fori_loop(0, dil * nblk, body, 0, unroll=DIL_UNROLL)
    for c in range(seq // 512):
        rows = slice(c * 512, (c + 1) * 512)
        o_ref[rows, :] = (acc_ref[rows, :] / l_ref[rows, :]).astype(BF16)


def _band_bias():
    q = jnp.arange(2 * DIL_QB, dtype=jnp.int32)[None, :, None] % DIL_QB
    k = jnp.arange(2 * DIL_QB, dtype=jnp.int32)[None, None, :]
    delta = jnp.array([0, -BAND, -DIL_QB], jnp.int32)[:, None, None]
    return jnp.where(jnp.abs(k + delta - q) <= BAND, 0.0, NEG_INF).astype(F32)


def _dilated_all(qa, ka, va, batch, seq):
    view = lambda a: a.reshape(batch, seq, A_WIDTH)
    spec = pl.BlockSpec((None, seq, LANES), lambda b, h: (b, 0, h))
    bias = _band_bias()
    out = pl.pallas_call(
        functools.partial(_dil_all_kernel, seq=seq),
        grid=(batch, A_WIDTH // LANES),
        in_specs=[pl.BlockSpec(bias.shape, lambda b, h: (0, 0, 0)), spec, spec, spec],
        out_specs=spec,
        out_shape=jax.ShapeDtypeStruct((batch, seq, A_WIDTH), BF16),
        scratch_shapes=[pltpu.VMEM((seq, LANES), F32)] * 3,
        compiler_params=pltpu.CompilerParams(dimension_semantics=("arbitrary",) * 2, vmem_limit_bytes=_vmem(48)),
        name="dilated",
    )(bias, view(qa), view(ka), view(va))
    return out.reshape(batch * seq, A_WIDTH)


def _post_kernel(xp_ref, oap_ref, obp_ref, xs_ref, oas_ref, obs_ref, wo_ref, g_ref, rwh_ref, rwl_ref, rb_ref,
                 x1_ref, hn_ref, route_ref, routet_ref, cout_ref, carry_ref, *, prompt_tiles):
    tm = xp_ref.shape[0]
    is_prompt = pl.program_id(0) < prompt_tiles

    @pl.when(pl.program_id(0) == 0)
    def _():
        carry_ref[...] = jnp.zeros_like(carry_ref)

    oa = jnp.where(is_prompt, oap_ref[...], oas_ref[...])
    ob = jnp.where(is_prompt, obp_ref[...], obs_ref[...])
    attn = jnp.dot(oa, wo_ref[:A_WIDTH, :], preferred_element_type=F32)
    attn += jnp.dot(ob, wo_ref[A_WIDTH:, :], preferred_element_type=F32)
    x1 = jnp.where(is_prompt, xp_ref[...], xs_ref[...]) + attn
    x1_ref[...] = x1
    hn = _rms(x1, g_ref[...])
    _store_token_tiles(hn_ref, hn)

    hi = hn.astype(BF16)
    lo = (hn - hi.astype(F32)).astype(BF16)
    logits = jnp.dot(hi, rwh_ref[...], preferred_element_type=F32)
    logits += jnp.dot(lo, rwh_ref[...], preferred_element_type=F32)
    logits += jnp.dot(hi, rwl_ref[...], preferred_element_type=F32)
    logits += rb_ref[...]

    lane = lax.broadcasted_iota(jnp.int32, (tm, LANES), 1)
    work = logits
    vals, sels = [], []
    for _ in range(TOP_K):
        mx = jnp.max(work, axis=-1, keepdims=True)
        first = jnp.min(jnp.where(work == mx, lane, LANES), axis=-1, keepdims=True)
        sel = lane == first
        work = jnp.where(sel, -jnp.inf, work)
        vals.append(mx)
        sels.append(sel)
    exps = [jnp.exp(v - vals[0]) for v in vals]
    den = exps[0] + exps[1] + exps[2] + exps[3]

    sel_all = (sels[0] | sels[1] | sels[2] | sels[3]).astype(F32)
    tri = (lax.broadcasted_iota(jnp.int32, (tm, tm), 1) < lax.broadcasted_iota(jnp.int32, (tm, tm), 0)).astype(BF16)
    before = jnp.dot(tri, sel_all.astype(BF16), preferred_element_type=F32) + carry_ref[0:1, :]
    carry_ref[0:1, :] = carry_ref[0:1, :] + jnp.sum(sel_all, axis=0, keepdims=True)
    cout_ref[...] = carry_ref[...]

    lane_f = lane.astype(F32)
    route = jnp.zeros((tm, LANES), F32)
    for k in range(TOP_K):
        idx = jnp.sum(jnp.where(sels[k], lane_f, 0.0), axis=-1, keepdims=True)
        rank = jnp.sum(jnp.where(sels[k], before, 0.0), axis=-1, keepdims=True)
        route = jnp.where(lane == k, idx, route)
        route = jnp.where(lane == TOP_K + k, rank, route)
        route = jnp.where(lane == 2 * TOP_K + k, exps[k] / den, route)
    route_ref[...] = route
    routet_ref[...] = route.T[:16, :]


def _post(prompt, sample, wts):
    tm = POST_TM
    n_p = prompt[0].shape[0] // tm
    n_s = sample[0].shape[0] // tm
    total = (n_p + n_s) * tm
    full = lambda a: pl.BlockSpec(a.shape, lambda i: (0,) * a.ndim)
    row_p = lambda w: pl.BlockSpec((tm, w), lambda i: (jnp.minimum(i, n_p - 1), 0))
    row_s = lambda w: pl.BlockSpec((tm, w), lambda i: (jnp.maximum(i - n_p, 0), 0))
    row = lambda w: pl.BlockSpec((tm, w), lambda i: (i, 0))
    widths = (D_MODEL, A_WIDTH, A_WIDTH)
    consts = [wts["w_o"], wts["ffn_norm"], wts["router_hi"], wts["router_lo"], wts["router_b"]]
    out_shape = [jax.ShapeDtypeStruct((total, D_MODEL), F32), jax.ShapeDtypeStruct((total * CHUNKS, LANES), F32),
                 jax.ShapeDtypeStruct((total, LANES), F32), jax.ShapeDtypeStruct((16, total), F32),
                 jax.ShapeDtypeStruct((8, LANES), F32)]
    out_specs = [row(D_MODEL), pl.BlockSpec((tm * CHUNKS, LANES), lambda i: (i, 0)), row(LANES),
                 pl.BlockSpec((16, tm), lambda i: (0, i)), pl.BlockSpec((8, LANES), lambda i: (0, 0))]
    return pl.pallas_call(
        functools.partial(_post_kernel, prompt_tiles=n_p),
        grid=(n_p + n_s,),
        in_specs=[row_p(w) for w in widths] + [row_s(w) for w in widths] + [full(c) for c in consts],
        out_specs=out_specs,
        out_shape=out_shape,
        scratch_shapes=[pltpu.VMEM((8, LANES), F32)],
        compiler_params=pltpu.CompilerParams(dimension_semantics=("arbitrary",), vmem_limit_bytes=_vmem(56)),
        name="post",
    )(*prompt, *sample, *consts)


def _dispatch_kernel(pos_ref, hn_ref, xs_ref, sem):
    tm = DISPATCH_TM

    def copy(t, slot):
        return pltpu.make_async_copy(hn_ref.at[t], xs_ref.at[slot], sem)

    def issue(t, c):
        for k in range(TOP_K):
            copy(t, pos_ref[0, 0, k * tm + t]).start(priority=k % 2)
        return c

    def drain(t, c):
        for k in range(TOP_K):
            copy(0, 0).wait()
        return c

    lax.fori_loop(0, tm, issue, 0, unroll=DMA_UNROLL)
    lax.fori_loop(0, tm, drain, 0, unroll=DMA_UNROLL)


def _dispatch(pos_blocks, hn, rows):
    tm = DISPATCH_TM
    hn3 = hn.reshape(-1, CHUNKS, LANES)
    return pl.pallas_call(
        _dispatch_kernel,
        grid=(hn3.shape[0] // tm,),
        in_specs=[pl.BlockSpec((1, 1, TOP_K * tm), lambda i: (i, 0, 0), memory_space=pltpu.SMEM),
                  pl.BlockSpec((tm, CHUNKS, LANES), lambda i: (i, 0, 0))],
        out_specs=pl.BlockSpec(memory_space=pl.ANY),
        out_shape=jax.ShapeDtypeStruct((rows, CHUNKS, LANES), F32),
        scratch_shapes=[pltpu.SemaphoreType.DMA(())],
        compiler_params=pltpu.CompilerParams(dimension_semantics=("arbitrary",)),
        name="dispatch",
    )(pos_blocks, hn3)


def _combine_kernel(pos_ref, pos_next_ref, ys_ref, x1_ref, route_ref, g_ref, o_ref, ybuf, sem, *, steps):
    tm = COMBINE_TM
    step = pl.program_id(0)
    cur = step % 2

    def copy(slot, buf, k, t):
        dst = ybuf.at[buf, k, pl.ds(pl.multiple_of(t * CHUNKS, CHUNKS), CHUNKS)]
        return pltpu.make_async_copy(ys_ref.at[slot], dst, sem.at[buf])

    def fetch(index_ref, buf):
        def issue(t, c):
            for k in range(TOP_K):
                copy(index_ref[0, 0, k * tm + t], buf, k, t).start(priority=k % 2)
            return c

        lax.fori_loop(0, tm, issue, 0, unroll=DMA_UNROLL)

    @pl.when(step == 0)
    def _():
        fetch(pos_ref, 0)

    @pl.when(step + 1 < steps)
    def _():
        fetch(pos_next_ref, 1 - cur)

    def drain(t, c):
        for k in range(TOP_K):
            copy(0, cur, k, 0).wait()
        return c

    lax.fori_loop(0, tm, drain, 0, unroll=DMA_UNROLL)
    route = route_ref[...]
    y = x1_ref[...]
    for k in range(TOP_K):
        y = y + route[:, 2 * TOP_K + k:2 * TOP_K + k + 1] * _load_token_tiles(ybuf.at[cur, k], tm)
    o_ref[...] = _rms(y, g_ref[...])


def _combine(pos_blocks, ys, x1, route, final_norm, row_off, rows):
    tm = COMBINE_TM
    off = row_off // tm
    last = off + rows // tm - 1
    pos_spec = lambda ahead: pl.BlockSpec((1, 1, TOP_K * tm), lambda i: (jnp.minimum(i + off + ahead, last), 0, 0),
                                          memory_space=pltpu.SMEM)
    return pl.pallas_call(
        functools.partial(_combine_kernel, steps=rows // tm),
        grid=(rows // tm,),
        in_specs=[pos_spec(0), pos_spec(1),
                  pl.BlockSpec(memory_space=pl.ANY),
                  pl.BlockSpec((tm, D_MODEL), lambda i: (i + off, 0)),
                  pl.BlockSpec((tm, LANES), lambda i: (i + off, 0)),
                  pl.BlockSpec((1, D_MODEL), lambda i: (0, 0))],
        out_specs=pl.BlockSpec((tm, D_MODEL), lambda i: (i, 0)),
        out_shape=jax.ShapeDtypeStruct((rows, D_MODEL), F32),
        scratch_shapes=[pltpu.VMEM((2, TOP_K, tm * CHUNKS, LANES), F32), pltpu.SemaphoreType.DMA((2,))],
        compiler_params=pltpu.CompilerParams(dimension_semantics=("arbitrary",), vmem_limit_bytes=_vmem(32)),
        name="combine",
    )(pos_blocks, pos_blocks, ys, x1, route, final_norm)


def _ffn_kernel(tile_ref, expert_ref, npairs_ref, x_ref, wgu_ref, bgu_ref, wd_ref, bd_ref, y_ref, wgu_bf, wd_bf):
    step = pl.program_id(0)
    n = FFN_TM // FFN_SUB

    @pl.when((step == 0) | (expert_ref[step] != expert_ref[jnp.maximum(step - 1, 0)]))
    def _():
        for r in range(0, D_MODEL, FFN_CAST_ROWS):
            rows = slice(r, r + FFN_CAST_ROWS)
            wgu_bf[rows, :] = wgu_ref[0, rows, :].astype(BF16)
            wd_bf[rows, :] = wd_ref[0, rows, :].astype(BF16)

    @pl.when(step >= npairs_ref[0])
    def _():
        y_ref[...] = jnp.zeros_like(y_ref)

    @pl.when(step < npairs_ref[0])
    def _():
        for h in range(FFN_SUB):
            xb = _load_token_tiles(x_ref, n, h * n).astype(BF16)
            gate = jnp.dot(xb, wgu_bf[:, :D_FF], preferred_element_type=F32) + bgu_ref[0, :, :D_FF]
            up = jnp.dot(xb, wgu_bf[:, D_FF:], preferred_element_type=F32) + bgu_ref[0, :, D_FF:]
            gate = jnp.minimum(gate, SWIGLU_LIMIT)
            up = jnp.clip(up, -SWIGLU_LIMIT, SWIGLU_LIMIT)
            act = (up + 1.0) * gate * (1.0 / (1.0 + jnp.exp(-SWIGLU_ALPHA * gate)))
            res = jnp.dot(act.astype(BF16), wd_bf[...], preferred_element_type=F32) + bd_ref[0]
            _store_token_tiles(y_ref, res, h * n)


def _ffn(pairs, xs, wts):
    tm = FFN_TM
    pair_tile, pair_expert, n_pairs = pairs
    slots = pair_tile.shape[0]
    x_map = lambda i, pt, pe, n: (pt[i], 0)
    w_map = lambda i, pt, pe, n: (pe[i], 0, 0)
    grid_spec = pltpu.PrefetchScalarGridSpec(
        num_scalar_prefetch=3,
        grid=(slots,),
        in_specs=[pl.BlockSpec((tm * CHUNKS, LANES), x_map),
                  pl.BlockSpec((1, D_MODEL, 2 * D_FF), w_map),
                  pl.BlockSpec((1, 1, 2 * D_FF), w_map),
                  pl.BlockSpec((1, D_FF, D_MODEL), w_map),
                  pl.BlockSpec((1, 1, D_MODEL), w_map)],
        out_specs=pl.BlockSpec((tm * CHUNKS, LANES), lambda i, pt, pe, n: (i, 0)),
        scratch_shapes=[pltpu.VMEM((D_MODEL, 2 * D_FF), BF16), pltpu.VMEM((D_FF, D_MODEL), BF16)],
    )
    ys = pl.pallas_call(
        _ffn_kernel,
        grid_spec=grid_spec,
        out_shape=jax.ShapeDtypeStruct((slots * tm * CHUNKS, LANES), F32),
        compiler_params=pltpu.CompilerParams(dimension_semantics=("arbitrary",), vmem_limit_bytes=_vmem(58)),
        name="ffn",
    )(pair_tile, pair_expert, n_pairs, xs.reshape(-1, LANES), wts["w_gate_up"], wts["b_gate_up"],
      wts["w_down"], wts["b_down"])
    return ys.reshape(slots * tm, CHUNKS, LANES)


def _rope_tables(seq):
    pos = jnp.arange(seq, dtype=F32)[:, None]

    def cs(dim):
        inv = 1.0 / (ROPE_THETA ** (jnp.arange(0, dim, 2, dtype=F32) / dim))
        ang = pos * inv[None, :]
        return jnp.cos(ang), jnp.sin(ang)

    ca, sa = cs(HEAD_DIM)
    cb, sb = cs(ROPE_DIM)
    one = jnp.ones((seq, NOPE_DIM), F32)
    pad1 = jnp.ones((seq, LANES - QK_DIM), F32)
    cosa = jnp.concatenate([ca, ca, ca, ca], axis=1)
    sina = jnp.concatenate([-sa, sa, -sa, sa], axis=1)
    cosb = jnp.concatenate([one, cb, cb, pad1], axis=1)
    sinb = jnp.concatenate([0.0 * one, -sb, sb, 0.0 * pad1], axis=1)
    return cosa, sina, cosb, sinb


def _prep_weights(attn_norm, w_in, q_norm, w_uq, kv_norm, w_ukv, w_o, ffn_norm, router_w, router_b,
                  w_gate_up, b_gate_up, w_down, b_down):
    w = w_in[0]
    zeros = lambda n: jnp.zeros((D_MODEL, n), F32)
    krope = jnp.concatenate([zeros(NOPE_DIM), w[:, 2176:], zeros(LANES - QK_DIM)], axis=1)
    w_in_p = jnp.concatenate([w[:, :2176], krope], axis=1).astype(BF16)
    uq = w_uq[0].reshape(Q_RANK, N_HEADS, QK_DIM)
    uq = jnp.pad(uq, ((0, 0), (0, 0), (0, LANES - QK_DIM))).reshape(Q_RANK, N_HEADS * LANES).astype(BF16)
    ukv = w_ukv[0].reshape(KV_RANK, N_HEADS, NOPE_DIM + V_DIM)
    pad = lambda a: jnp.pad(a, ((0, 0), (0, 0), (0, LANES - a.shape[2]))).reshape(KV_RANK, N_HEADS * LANES)
    ukv = jnp.concatenate([pad(ukv[:, :, :NOPE_DIM]), pad(ukv[:, :, NOPE_DIM:])], axis=1).astype(BF16)
    rw = jnp.pad(router_w[0], ((0, 0), (0, LANES - N_EXPERTS)))
    rw_hi = rw.astype(BF16)
    rw_lo = (rw - rw_hi.astype(F32)).astype(BF16)
    rb = jnp.concatenate([router_b[0], jnp.full((LANES - N_EXPERTS,), NEG_INF, F32)])[None, :]
    return {
        "attn_norm": attn_norm[0][None, :], "w_in": w_in_p, "q_norm": q_norm[0][None, :], "w_uq": uq,
        "kv_norm": kv_norm[0][None, :], "w_ukv": ukv, "w_o": w_o[0].astype(BF16), "ffn_norm": ffn_norm[0][None, :],
        "router_hi": rw_hi, "router_lo": rw_lo, "router_b": rb,
        "w_gate_up": w_gate_up[0], "b_gate_up": b_gate_up[0][:, None, :],
        "w_down": w_down[0], "b_down": b_down[0][:, None, :],
    }


def _routing(route_t, counts, total):
    i32 = jnp.int32
    idx = route_t[0:TOP_K].astype(i32)
    rank = route_t[TOP_K:2 * TOP_K].astype(i32)
    cnt = counts[0, :N_EXPERTS].astype(i32)
    seg_end = jnp.cumsum(cnt)
    seg_start = seg_end - cnt
    experts = jnp.arange(N_EXPERTS, dtype=i32)
    per_expert = lambda table: jnp.sum(jnp.where(idx[None] == experts[:, None, None], table[:, None, None], 0), axis=0)
    pos = rank + per_expert(seg_start)

    first_tile = seg_start // FFN_TM
    n_per = jnp.where(cnt > 0, (seg_end - 1) // FFN_TM - first_tile + 1, 0)
    pair_end = jnp.cumsum(n_per)
    n_pairs = pair_end[-1]
    pos_out = pos + per_expert((pair_end - n_per - first_tile) * FFN_TM)
    slots = total * TOP_K // FFN_TM + N_EXPERTS
    j = jnp.minimum(jnp.arange(slots, dtype=i32), n_pairs - 1)
    onehot = (jnp.sum((j[:, None] >= pair_end[None, :]).astype(i32), axis=1)[:, None] == experts[None, :]).astype(i32)
    pick = lambda table: jnp.sum(onehot * table[None, :], axis=1)
    pair_expert = pick(experts)
    pair_tile = pick(first_tile) + j - pick(pair_end - n_per)

    def blocks(p, tm):
        return p.reshape(TOP_K, total // tm, tm).transpose(1, 0, 2).reshape(total // tm, 1, TOP_K * tm)

    return blocks(pos, DISPATCH_TM), blocks(pos_out, COMBINE_TM), (pair_tile, pair_expert, n_pairs[None].astype(i32))


def _mixers(x, wts):
    batch, seq, _ = x.shape
    x2d = x.reshape(batch * seq, D_MODEL)
    qa, ka, va, qb, kb, vbt = _proj(x2d, seq, wts, _rope_tables(seq))
    oa = _dilated_all(qa, ka, va, batch, seq)
    ob = _attn_b(qb, kb, vbt, batch, seq)
    return x2d, oa, ob


def kernel(x_prompt, x_sample, attn_norm, w_in, q_norm, w_uq, kv_norm, w_ukv, w_o, ffn_norm, router_w, router_b,
           w_gate_up, b_gate_up, w_down, b_down, final_norm):
    wts = _prep_weights(attn_norm, w_in, q_norm, w_uq, kv_norm, w_ukv, w_o, ffn_norm, router_w, router_b,
                        w_gate_up, b_gate_up, w_down, b_down)
    sets = [x_prompt, x_sample]
    rows = [x.shape[0] * x.shape[1] for x in sets]
    total = sum(rows)
    x1, hn, route, route_t, counts = _post(_mixers(x_prompt, wts), _mixers(x_sample, wts), wts)
    pos_dispatch, pos_combine, pairs = _routing(route_t, counts, total)
    xs = _dispatch(pos_dispatch, hn, total * TOP_K)
    ys = _ffn(pairs, xs, wts)
    outs = []
    off = 0
    fnorm = final_norm[None, :]
    for x, n in zip(sets, rows):
        outs.append(_combine(pos_combine, ys, x1, route, fnorm, off, n).reshape(x.shape))
        off += n
    return tuple(outs)
```
